```python
import math, functools
import jax, jax.numpy as jnp
from jax import lax
import numpy as np

D_MODEL = 2048
BATCH = 4
SEQ = 4096
DEPTH = 2

N_META = 16
N_MIXERS = 2
N_SC_LAYERS = (DEPTH + 1) // 2
N_GDN_LAYERS = DEPTH // 2
SC_WIDTH = 3
GDN_NK = 16
GDN_NV = 32
GDN_DK = 128
GDN_DV = 128
GDN_QK_DIM = GDN_NK * GDN_DK
GDN_V_DIM = GDN_NV * GDN_DV
GDN_CONV_DIM = 2 * GDN_QK_DIM + GDN_V_DIM
GDN_PROJ_DIM = GDN_CONV_DIM + GDN_V_DIM + 2 * GDN_NV
GDN_CONV_WIDTH = 4
CHUNK = 64
D_FF = ((math.ceil(8 * D_MODEL / 3) + 255) // 256) * 256
NORM_EPS = 1e-6
L2_EPS = 1e-6

kernel_name = "hybrid_shortconv_gdn_meta"


def rmsnorm(x, w):
    xf = x.astype(jnp.float32)
    y = xf * lax.rsqrt(jnp.mean(xf * xf, axis=-1, keepdims=True) + NORM_EPS)
    return (y * w.astype(jnp.float32)).astype(x.dtype)


def l2norm(x):
    return x * lax.rsqrt(jnp.sum(x * x, axis=-1, keepdims=True) + L2_EPS)


def causal_depthwise_conv(u, w):
    width = w.shape[0]
    length = u.shape[1]
    up = jnp.pad(u, ((0, 0), (width - 1, 0), (0, 0)))
    return sum(w[j] * up[:, j:j + length] for j in range(width))


def short_conv_mixer(h, w_in, conv_w, w_out):
    b_gate, c_gate, u = jnp.split(h @ w_in, 3, axis=-1)
    y = b_gate * causal_depthwise_conv(c_gate * u, conv_w)
    return y @ w_out


def chunked_gated_delta_rule(q, k, v, g, beta):
    bsz, length, nh, dk = q.shape
    dv = v.shape[-1]
    pad_front = (-N_META) % CHUNK
    pad_back = (-(length + pad_front)) % CHUNK
    n = (length + pad_front + pad_back) // CHUNK

    def prep(t):
        t = jnp.pad(t, [(0, 0), (pad_front, pad_back)] + [(0, 0)] * (t.ndim - 2))
        t = t.reshape((bsz, n, CHUNK) + t.shape[2:])
        return jnp.moveaxis(t, 3, 1)

    q, k, v, g, beta = map(prep, (q, k, v, g, beta))
    q = q * (dk ** -0.5)
    g = jnp.cumsum(g, axis=-1)
    causal = jnp.tril(jnp.ones((CHUNK, CHUNK), dtype=bool))
    strict = jnp.tril(jnp.ones((CHUNK, CHUNK), dtype=bool), k=-1)
    decay = jnp.exp(jnp.where(causal, g[..., :, None] - g[..., None, :], -jnp.inf))

    k_beta = k * beta[..., None]
    v_beta = v * beta[..., None]
    a_mat = jnp.where(strict, jnp.einsum('bhncd,bhnsd->bhncs', k_beta, k) * decay, 0.0)
    eye = jnp.eye(CHUNK, dtype=a_mat.dtype)
    t_mat = lax.linalg.triangular_solve(eye + a_mat, jnp.broadcast_to(eye, a_mat.shape),
                                        left_side=True, lower=True)
    u_c = jnp.einsum('bhncs,bhnsv->bhncv', t_mat, v_beta)
    w_c = jnp.einsum('bhncs,bhnsd->bhncd', t_mat, k_beta * jnp.exp(g)[..., None])
    qk = jnp.einsum('bhncd,bhnsd->bhncs', q, k) * decay
    q_g = q * jnp.exp(g)[..., None]
    k_tail = k * jnp.exp(g[..., -1:] - g)[..., None]
    g_last = jnp.exp(g[..., -1])

    def step(state, inp):
        w_i, u_i, qg_i, qk_i, kt_i, gl_i = inp
        v_new = u_i - jnp.einsum('bhcd,bhdv->bhcv', w_i, state)
        o_i = jnp.einsum('bhcd,bhdv->bhcv', qg_i, state) + jnp.einsum('bhcs,bhsv->bhcv', qk_i, v_new)
        state = state * gl_i[..., None, None] + jnp.einsum('bhcd,bhcv->bhdv', kt_i, v_new)
        return state, o_i

    xs = tuple(jnp.moveaxis(t, 2, 0) for t in (w_c, u_c, q_g, qk, k_tail, g_last))
    s0 = jnp.zeros((bsz, nh, dk, dv), jnp.float32)
    _, o = lax.scan(step, s0, xs)
    o = jnp.transpose(o, (1, 0, 3, 2, 4)).reshape(bsz, n * CHUNK, nh, dv)
    return o[:, pad_front:pad_front + length]


def gated_deltanet_mixer(h, w_in, conv_w, a_log, dt_bias, norm_w, w_out):
    bsz, length, _ = h.shape
    proj = h @ w_in
    qkv, z, b, a = jnp.split(proj, [GDN_CONV_DIM, GDN_CONV_DIM + GDN_V_DIM,
                                    GDN_CONV_DIM + GDN_V_DIM + GDN_NV], axis=-1)
    qkv = jax.nn.silu(causal_depthwise_conv(qkv, conv_w))
    q, k, v = jnp.split(qkv.astype(jnp.float32), [GDN_QK_DIM, 2 * GDN_QK_DIM], axis=-1)
    rep = GDN_NV // GDN_NK
    q = jnp.repeat(l2norm(q.reshape(bsz, length, GDN_NK, GDN_DK)), rep, axis=2)
    k = jnp.repeat(l2norm(k.reshape(bsz, length, GDN_NK, GDN_DK)), rep, axis=2)
    v = v.reshape(bsz, length, GDN_NV, GDN_DV)
    beta = jax.nn.sigmoid(b.astype(jnp.float32))
    g = -jnp.exp(a_log.astype(jnp.float32)) * jax.nn.softplus(
        a.astype(jnp.float32) + dt_bias.astype(jnp.float32))
    o = chunked_gated_delta_rule(q, k, v, g, beta)
    zf = z.astype(jnp.float32).reshape(bsz, length, GDN_NV, GDN_DV)
    o = rmsnorm(o, norm_w) * jax.nn.silu(zf)
    return o.reshape(bsz, length, GDN_V_DIM).astype(h.dtype) @ w_out


def swiglu_ffn(h, w_gate, w_up, w_down):
    return (jax.nn.silu(h @ w_gate) * (h @ w_up)) @ w_down


def setup_inputs(seed: int = 0) -> dict:
    key = jax.random.key(seed)
    ks = jax.random.split(key, 20)
    f32 = jnp.float32

    def nrm(k, shape, fan_in):
        return jax.random.normal(k, shape, f32) * (fan_in ** -0.5)

    def gain(k, shape):
        return 1.0 + 0.02 * jax.random.normal(k, shape, f32)

    return {
        "x": jax.random.normal(ks[0], (BATCH, SEQ, D_MODEL), f32),
        "meta_tokens": jax.random.normal(ks[1], (N_META, D_MODEL), f32),
        "mixer_norm": gain(ks[2], (DEPTH, D_MODEL)),
        "ffn_norm": gain(ks[3], (DEPTH, D_MODEL)),
        "sc_w_in": nrm(ks[4], (N_SC_LAYERS, D_MODEL, 3 * D_MODEL), D_MODEL),
        "sc_conv_w": nrm(ks[5], (N_SC_LAYERS, SC_WIDTH, D_MODEL), SC_WIDTH),
        "sc_w_out": nrm(ks[6], (N_SC_LAYERS, D_MODEL, D_MODEL), D_MODEL),
        "gdn_w_in": nrm(ks[7], (N_GDN_LAYERS, D_MODEL, GDN_PROJ_DIM), D_MODEL),
        "gdn_conv_w": nrm(ks[8], (N_GDN_LAYERS, GDN_CONV_WIDTH, GDN_CONV_DIM), GDN_CONV_WIDTH),
        "gdn_a_log": jnp.log(jax.random.uniform(ks[9], (N_GDN_LAYERS, GDN_NV), f32, 1.0, 16.0)),
        "gdn_dt_bias": 1.0 + 0.1 * jax.random.normal(ks[10], (N_GDN_LAYERS, GDN_NV), f32),
        "gdn_norm_w": gain(ks[11], (N_GDN_LAYERS, GDN_DV)),
        "gdn_w_out": nrm(ks[12], (N_GDN_LAYERS, GDN_V_DIM, D_MODEL), GDN_V_DIM),
        "ffn_w_gate": nrm(ks[13], (DEPTH, D_MODEL, D_FF), D_MODEL),
        "ffn_w_up": nrm(ks[14], (DEPTH, D_MODEL, D_FF), D_MODEL),
        "ffn_w_down": nrm(ks[15], (DEPTH, D_FF, D_MODEL), D_FF),
        "final_norm": gain(ks[16], (D_MODEL,)),
    }


def reference(x, meta_tokens, mixer_norm, ffn_norm, sc_w_in, sc_conv_w, sc_w_out,
              gdn_w_in, gdn_conv_w, gdn_a_log, gdn_dt_bias, gdn_norm_w, gdn_w_out,
              ffn_w_gate, ffn_w_up, ffn_w_down, final_norm):
    bsz = x.shape[0]
    meta = jnp.broadcast_to(meta_tokens.astype(x.dtype)[None], (bsz, N_META, D_MODEL))
    h = jnp.concatenate([meta, x], axis=1)
    for i in range(DEPTH):
        j = i // N_MIXERS
        hn = rmsnorm(h, mixer_norm[i])
        if i % N_MIXERS == 0:
            h = h + short_conv_mixer(hn, sc_w_in[j], sc_conv_w[j], sc_w_out[j])
        else:
            h = h + gated_deltanet_mixer(hn, gdn_w_in[j], gdn_conv_w[j], gdn_a_log[j],
                                         gdn_dt_bias[j], gdn_norm_w[j], gdn_w_out[j])
        h = h + swiglu_ffn(rmsnorm(h, ffn_norm[i]), ffn_w_gate[i], ffn_w_up[i], ffn_w_down[i])
    return rmsnorm(h, final_norm)[:, N_META:]
```

```python
import functools
import math

import jax
import jax.numpy as jnp
from jax import lax
from jax.experimental import pallas as pl
from jax.experimental.pallas import tpu as pltpu

F32 = jnp.float32
BF16 = jnp.bfloat16

N_META = 16
SC_WIDTH = 3
GDN_NK = 16
GDN_NV = 32
GDN_DK = 128
GDN_DV = 128
GDN_QK_DIM = GDN_NK * GDN_DK
GDN_V_DIM = GDN_NV * GDN_DV
GDN_CONV_DIM = 2 * GDN_QK_DIM + GDN_V_DIM
GDN_CONV_WIDTH = 4
CHUNK = 64
NORM_EPS = 1e-6
L2_EPS = 1e-6
PAD_FRONT = (-N_META) % CHUNK

LANES = 128
SUBLANES = 8
BF16_ROWS = 16
VMEM_BYTES_V7X = 64 * 1024 * 1024
VMEM_HEADROOM = 8 * 1024 * 1024

HEADS_PER_STEP = 4
KHEADS_PER_STEP = HEADS_PER_STEP * GDN_NK // GDN_NV
HALO = SUBLANES


def _row_tile(t, target, mult=BF16_ROWS):
    best = None
    for d in range(mult, min(t, target) + 1, mult):
        if t % d == 0:
            best = d
    assert best is not None, (t, target, mult)
    return best


def _col_tile(n, target):
    best = None
    for d in range(LANES, min(n, target) + 1, LANES):
        if n % d == 0:
            best = d
    assert best is not None, (n, target)
    return best


def _nbytes(shape, dtype):
    return math.prod(shape) * jnp.dtype(dtype).itemsize


def _params(semantics, block_bytes, scratch_bytes=0):
    need = 2 * block_bytes + scratch_bytes + VMEM_HEADROOM
    return pltpu.CompilerParams(
        dimension_semantics=semantics,
        vmem_limit_bytes=min(max(need, 32 * 1024 * 1024), VMEM_BYTES_V7X - 4 * 1024 * 1024),
    )


def _rmsnorm_kernel(h_ref, w_ref, o_ref):
    x = h_ref[...]
    y = x * lax.rsqrt(jnp.mean(x * x, axis=-1, keepdims=True) + NORM_EPS)
    o_ref[...] = (y * w_ref[...]).astype(o_ref.dtype)


def _rmsnorm(h, w, out_dtype, t):
    m, d = h.shape
    tr = _row_tile(t, 512)
    blocks = _nbytes((tr, d), F32) + _nbytes((tr, d), out_dtype)
    return pl.pallas_call(
        _rmsnorm_kernel,
        grid=(m // tr,),
        in_specs=[pl.BlockSpec((tr, d), lambda i: (i, 0)),
                  pl.BlockSpec((1, d), lambda i: (0, 0))],
        out_specs=pl.BlockSpec((tr, d), lambda i: (i, 0)),
        out_shape=jax.ShapeDtypeStruct((m, d), out_dtype),
        compiler_params=_params(("arbitrary",), blocks),
        name="rmsnorm",
    )(h, w.reshape(1, d).astype(F32))


def _cast_weight(w_ref, wb_ref):
    @pl.when(pl.program_id(1) == 0)
    def _():
        wb_ref[...] = w_ref[...].astype(wb_ref.dtype)


def _mm_plain_kernel(x_ref, w_ref, o_ref, wb_ref):
    _cast_weight(w_ref, wb_ref)
    acc = jnp.dot(x_ref[...], wb_ref[...], preferred_element_type=F32)
    o_ref[...] = acc.astype(o_ref.dtype)


def _mm_residual_kernel(x_ref, w_ref, r_ref, o_ref, wb_ref):
    _cast_weight(w_ref, wb_ref)
    acc = jnp.dot(x_ref[...], wb_ref[...], preferred_element_type=F32)
    o_ref[...] = r_ref[...] + acc


def _mm_swiglu_kernel(x_ref, wg_ref, wu_ref, o_ref, wgb_ref, wub_ref):
    _cast_weight(wg_ref, wgb_ref)
    _cast_weight(wu_ref, wub_ref)
    x = x_ref[...]
    g = jnp.dot(x, wgb_ref[...], preferred_element_type=F32)
    u = jnp.dot(x, wub_ref[...], preferred_element_type=F32)
    o_ref[...] = (g * jax.nn.sigmoid(g) * u).astype(o_ref.dtype)


def _mm_tiles(t, k, n, n_weights):
    tn = _col_tile(n, 1024 if k <= 2048 and n_weights == 1 else 512)
    tm = _row_tile(t, 1040 if k <= 4096 else 640)
    return tm, tn


def _mm_plain(x, w, out_dtype, t, n_cols=None, col_block_offset=0):
    m, k = x.shape
    n = w.shape[1] if n_cols is None else n_cols
    tm, tn = _mm_tiles(t, k, n, 1)
    off = col_block_offset
    blocks = (_nbytes((tm, k), BF16) + _nbytes((k, tn), F32) + _nbytes((tm, tn), out_dtype))
    return pl.pallas_call(
        _mm_plain_kernel,
        grid=(n // tn, m // tm),
        in_specs=[pl.BlockSpec((tm, k), lambda j, i: (i, 0)),
                  pl.BlockSpec((k, tn), lambda j, i: (0, j + off))],
        out_specs=pl.BlockSpec((tm, tn), lambda j, i: (i, j)),
        out_shape=jax.ShapeDtypeStruct((m, n), out_dtype),
        scratch_shapes=[pltpu.VMEM((k, tn), BF16)],
        compiler_params=_params(("arbitrary", "arbitrary"), blocks, _nbytes((k, tn), BF16)),
        name="mm_plain",
    )(x, w)


def _mm_residual(x, w, res, t):
    m, k = x.shape
    n = w.shape[1]
    tm, tn = _mm_tiles(t, k, n, 1)
    blocks = (_nbytes((tm, k), BF16) + _nbytes((k, tn), F32) + 2 * _nbytes((tm, tn), F32))
    return pl.pallas_call(
        _mm_residual_kernel,
        grid=(n // tn, m // tm),
        in_specs=[pl.BlockSpec((tm, k), lambda j, i: (i, 0)),
                  pl.BlockSpec((k, tn), lambda j, i: (0, j)),
                  pl.BlockSpec((tm, tn), lambda j, i: (i, j))],
        out_specs=pl.BlockSpec((tm, tn), lambda j, i: (i, j)),
        out_shape=jax.ShapeDtypeStruct((m, n), F32),
        scratch_shapes=[pltpu.VMEM((k, tn), BF16)],
        compiler_params=_params(("arbitrary", "arbitrary"), blocks, _nbytes((k, tn), BF16)),
        name="mm_residual",
    )(x, w, res)


def _mm_swiglu(x, wg, wu, t):
    m, k = x.shape
    n = wg.shape[1]
    tm, tn = _mm_tiles(t, k, n, 2)
    blocks = (_nbytes((tm, k), BF16) + 2 * _nbytes((k, tn), F32) + _nbytes((tm, tn), BF16))
    return pl.pallas_call(
        _mm_swiglu_kernel,
        grid=(n // tn, m // tm),
        in_specs=[pl.BlockSpec((tm, k), lambda j, i: (i, 0)),
                  pl.BlockSpec((k, tn), lambda j, i: (0, j)),
                  pl.BlockSpec((k, tn), lambda j, i: (0, j))],
        out_specs=pl.BlockSpec((tm, tn), lambda j, i: (i, j)),
        out_shape=jax.ShapeDtypeStruct((m, n), BF16),
        scratch_shapes=[pltpu.VMEM((k, tn), BF16), pltpu.VMEM((k, tn), BF16)],
        compiler_params=_params(("arbitrary", "arbitrary"), blocks, 2 * _nbytes((k, tn), BF16)),
        name="mm_swiglu",
    )(x, wg, wu)


def _causal_conv(x, carry_ref, w, first_tile):
    @pl.when(first_tile)
    def _():
        carry_ref[...] = jnp.zeros_like(carry_ref)

    r = x.shape[0]
    width = w.shape[0]
    xe = jnp.concatenate([carry_ref[...], x], axis=0)
    y = w[width - 1:width] * x
    for s in range(1, width):
        y = y + w[width - 1 - s:width - s] * xe[HALO - s:HALO - s + r]
    carry_ref[...] = x[r - HALO:]
    return y


def _sconv_kernel(b_ref, c_ref, u_ref, w_ref, o_ref, carry_ref, *, tiles_per_seq):
    cu = c_ref[...].astype(F32) * u_ref[...].astype(F32)
    y = _causal_conv(cu, carry_ref, w_ref[...], pl.program_id(1) % tiles_per_seq == 0)
    o_ref[...] = (b_ref[...].astype(F32) * y).astype(o_ref.dtype)


def _sconv_gate(bcu, conv_w, t):
    m = bcu.shape[0]
    d = bcu.shape[1] // 3
    tr = _row_tile(t, 1040)
    tc = _col_tile(d, 512)
    nb = d // tc
    blocks = 4 * _nbytes((tr, tc), BF16) + _nbytes((SC_WIDTH, tc), F32)
    return pl.pallas_call(
        functools.partial(_sconv_kernel, tiles_per_seq=t // tr),
        grid=(nb, m // tr),
        in_specs=[pl.BlockSpec((tr, tc), lambda j, i: (i, j)),
                  pl.BlockSpec((tr, tc), lambda j, i: (i, j + nb)),
                  pl.BlockSpec((tr, tc), lambda j, i: (i, j + 2 * nb)),
                  pl.BlockSpec((SC_WIDTH, tc), lambda j, i: (0, j))],
        out_specs=pl.BlockSpec((tr, tc), lambda j, i: (i, j)),
        out_shape=jax.ShapeDtypeStruct((m, d), BF16),
        scratch_shapes=[pltpu.VMEM((HALO, tc), F32)],
        compiler_params=_params(("arbitrary", "arbitrary"), blocks, 16 * _nbytes((tr, tc), F32)),
        name="sconv_gate",
    )(bcu, bcu, bcu, conv_w.astype(F32))


def _gdn_conv_kernel(x_ref, w_ref, o_ref, carry_ref, *, tiles_per_seq, n_l2_blocks, n_q_blocks):
    j = pl.program_id(0)
    x = x_ref[...].astype(F32)
    y = _causal_conv(x, carry_ref, w_ref[...], pl.program_id(1) % tiles_per_seq == 0)
    y = y * jax.nn.sigmoid(y)

    @pl.when(j < n_l2_blocks)
    def _():
        scale = jnp.where(j < n_q_blocks, GDN_DK ** -0.5, 1.0).astype(F32)
        for hh in range(y.shape[1] // GDN_DK):
            ys = y[:, hh * GDN_DK:(hh + 1) * GDN_DK]
            inv = lax.rsqrt(jnp.sum(ys * ys, axis=-1, keepdims=True) + L2_EPS) * scale
            o_ref[:, hh * GDN_DK:(hh + 1) * GDN_DK] = (ys * inv).astype(o_ref.dtype)

    @pl.when(j >= n_l2_blocks)
    def _():
        o_ref[...] = y.astype(o_ref.dtype)


def _gdn_conv(proj, conv_w, t, out_dtype):
    m = proj.shape[0]
    tr = _row_tile(t, 1040)
    tc = 512
    blocks = _nbytes((tr, tc), proj.dtype) + _nbytes((tr, tc), out_dtype) + _nbytes((GDN_CONV_WIDTH, tc), F32)
    return pl.pallas_call(
        functools.partial(_gdn_conv_kernel, tiles_per_seq=t // tr,
                          n_l2_blocks=2 * GDN_QK_DIM // tc, n_q_blocks=GDN_QK_DIM // tc),
        grid=(GDN_CONV_DIM // tc, m // tr),
        in_specs=[pl.BlockSpec((tr, tc), lambda j, i: (i, j)),
                  pl.BlockSpec((GDN_CONV_WIDTH, tc), lambda j, i: (0, j))],
        out_specs=pl.BlockSpec((tr, tc), lambda j, i: (i, j)),
        out_shape=jax.ShapeDtypeStruct((m, GDN_CONV_DIM), out_dtype),
        scratch_shapes=[pltpu.VMEM((HALO, tc), F32)],
        compiler_params=_params(("arbitrary", "arbitrary"), blocks, 16 * _nbytes((tr, tc), F32)),
        name="gdn_conv",
    )(proj, conv_w.astype(F32))


def _gdn_gates_kernel(ba_ref, alog_ref, dtb_ref, o_ref):
    x = ba_ref[...]
    r = x.shape[0]
    z = x + dtb_ref[...]
    softplus = jnp.maximum(z, 0.0) + jnp.log1p(jnp.exp(-jnp.abs(z)))
    g = -jnp.exp(alog_ref[...]) * softplus
    pos = lax.broadcasted_iota(jnp.int32, x.shape, 0) % CHUNK
    s = 1
    while s < CHUNK:
        g = g + jnp.where(pos >= s, pltpu.roll(g, s, 0), 0.0)
        s *= 2
    lane = lax.broadcasted_iota(jnp.int32, x.shape, 1)
    o_ref[...] = jnp.where(lane < GDN_NV, jax.nn.sigmoid(x), g)


def _gdn_gates(ba, a_log, dt_bias, t):
    m = ba.shape[0]
    tr = _row_tile(t, 1040, CHUNK)
    pad = LANES - 2 * GDN_NV
    alog = jnp.concatenate([jnp.zeros((GDN_NV,), F32), a_log.astype(F32), jnp.zeros((pad,), F32)])
    dtb = jnp.concatenate([jnp.zeros((GDN_NV,), F32), dt_bias.astype(F32), jnp.zeros((pad,), F32)])
    return pl.pallas_call(
        _gdn_gates_kernel,
        grid=(m // tr,),
        in_specs=[pl.BlockSpec((tr, LANES), lambda i: (i, 0)),
                  pl.BlockSpec((1, LANES), lambda i: (0, 0)),
                  pl.BlockSpec((1, LANES), lambda i: (0, 0))],
        out_specs=pl.BlockSpec((tr, LANES), lambda i: (i, 0)),
        out_shape=jax.ShapeDtypeStruct((m, LANES), F32),
        compiler_params=_params(("arbitrary",), 2 * _nbytes((tr, LANES), F32)),
        name="gdn_gates",
    )(ba, alog.reshape(1, LANES), dtb.reshape(1, LANES))


def _bdot(a, b):
    return jnp.dot(a.astype(BF16), b.astype(BF16), preferred_element_type=F32)


def _gdn_chunk_kernel(q_ref, k_ref, v_ref, z_ref, col_ref, row_ref, nw_ref, o_ref, s_ref):
    n_chunks = q_ref.shape[0] // CHUNK
    rep = HEADS_PER_STEP // KHEADS_PER_STEP

    @pl.when(pl.program_id(2) == 0)
    def _():
        s_ref[...] = jnp.zeros_like(s_ref)

    ci = lax.broadcasted_iota(jnp.int32, (CHUNK, CHUNK), 0)
    si = lax.broadcasted_iota(jnp.int32, (CHUNK, CHUNK), 1)
    causal = ci >= si
    strict = ci > si
    blk16 = (ci >> 4) == (si >> 4)
    blk32 = (ci >> 5) == (si >> 5)
    eye = (ci == si).astype(F32)
    lane2 = lax.broadcasted_iota(jnp.int32, (CHUNK, 2 * CHUNK), 1)
    upper_half = lane2 >= CHUNK
    nw = nw_ref[...]

    def chunk_body(c, carry):
        r0 = pl.multiple_of(c * CHUNK, CHUNK)
        rows = pl.ds(r0, CHUNK)
        for kh in range(KHEADS_PER_STEP):
            kcols = slice(kh * GDN_DK, (kh + 1) * GDN_DK)
            qc = q_ref[rows, kcols].astype(F32)
            kc = k_ref[rows, kcols].astype(F32)
            heads = [kh * rep + r for r in range(rep)]
            beta = [col_ref[0, rows, h:h + 1] for h in heads]
            gcol = [col_ref[0, rows, HEADS_PER_STEP + h:HEADS_PER_STEP + h + 1] for h in heads]
            grow = [row_ref[0, c, h:h + 1, :] for h in heads]
            kb = [kc * b for b in beta]
            lhs = jnp.concatenate(kb + [qc], axis=0).astype(BF16)
            kk = lax.dot_general(lhs, kc.astype(BF16), (((1,), (1,)), ((), ())),
                                 preferred_element_type=F32)
            qk_raw = kk[rep * CHUNK:]
            for r, h in enumerate(heads):
                vcols = slice(h * GDN_DV, (h + 1) * GDN_DV)
                vc = v_ref[rows, vcols].astype(F32)
                glast = gcol[r][CHUNK - 1:CHUNK, :]
                eg = jnp.exp(gcol[r])
                etail = jnp.exp(glast - gcol[r])
                decay = jnp.where(causal, jnp.exp(gcol[r] - grow[r]), 0.0)
                nmat = jnp.where(strict, -kk[r * CHUNK:(r + 1) * CHUNK] * decay, 0.0)
                nd = jnp.where(blk16, nmat, 0.0)
                no1 = jnp.where(blk32, nmat, 0.0) - nd
                no2 = jnp.where(blk32, 0.0, nmat)
                s1 = eye + nd
                p1 = _bdot(nd, nd)
                ps = jnp.concatenate([p1, s1], axis=1)
                for _ in range(2):
                    ps = _bdot(ps[:, :CHUNK], ps) + jnp.where(upper_half, ps, 0.0)
                ts = _bdot(ps[:, :CHUNK], ps) + ps
                t16 = ts[:, CHUNK:]
                t32 = t16 + _bdot(t16, _bdot(no1, t16))
                tm = t32 - eye
                rhs = jnp.concatenate([vc * beta[r], kb[r] * eg], axis=1)
                y = rhs + _bdot(tm, rhs)
                zz = _bdot(no2, y)
                uw = y + zz + _bdot(tm, zz)
                u = uw[:, :GDN_DV]
                w = uw[:, GDN_DV:]
                s = s_ref[h]
                ws_qs = _bdot(jnp.concatenate([w, qc * eg], axis=0), s)
                v_new = u - ws_qs[:CHUNK]
                qk = jnp.where(causal, qk_raw * decay, 0.0)
                o = ws_qs[CHUNK:] + _bdot(qk, v_new)
                ds = lax.dot_general((kc * etail).astype(BF16), v_new.astype(BF16),
                                     (((0,), (0,)), ((), ())), preferred_element_type=F32)
                s_ref[h] = s * jnp.exp(glast) + ds
                zc = z_ref[rows, vcols].astype(F32)
                on = o * lax.rsqrt(jnp.mean(o * o, axis=-1, keepdims=True) + NORM_EPS) * nw
                o_ref[rows, vcols] = (on * (zc * jax.nn.sigmoid(zc))).astype(o_ref.dtype)
        return carry

    lax.fori_loop(0, n_chunks, chunk_body, 0)


def _gdn_chunk(qkv, proj, gates, norm_w, bsz, t):
    m = qkv.shape[0]
    hs, khs = HEADS_PER_STEP, KHEADS_PER_STEP
    n_groups = GDN_NV // hs
    rb = _row_tile(t, 1040, CHUNK)
    nrb = t // rb
    ncb = rb // CHUNK
    n_chunks_total = m // CHUNK
    beta = gates[:, :GDN_NV].reshape(m, n_groups, hs)
    gc = gates[:, GDN_NV:2 * GDN_NV].reshape(m, n_groups, hs)
    col = jnp.transpose(jnp.concatenate([beta, gc], axis=-1), (1, 0, 2))
    row = jnp.transpose(gc.reshape(n_chunks_total, CHUNK, n_groups, hs), (2, 0, 3, 1))

    kq = khs * GDN_DK
    vq = hs * GDN_DV
    q_off = 0
    k_off = GDN_QK_DIM // kq
    v_off = 2 * GDN_QK_DIM // vq
    z_off = GDN_CONV_DIM // vq
    blocks = (2 * _nbytes((rb, kq), qkv.dtype) + _nbytes((rb, vq), qkv.dtype)
              + _nbytes((rb, vq), proj.dtype) + _nbytes((rb, LANES), F32)
              + _nbytes((ncb, SUBLANES, LANES), F32) + _nbytes((rb, vq), BF16))
    return pl.pallas_call(
        _gdn_chunk_kernel,
        grid=(bsz, n_groups, nrb),
        in_specs=[pl.BlockSpec((rb, kq), lambda b, g, r: (b * nrb + r, q_off + g)),
                  pl.BlockSpec((rb, kq), lambda b, g, r: (b * nrb + r, k_off + g)),
                  pl.BlockSpec((rb, vq), lambda b, g, r: (b * nrb + r, v_off + g)),
                  pl.BlockSpec((rb, vq), lambda b, g, r: (b * nrb + r, z_off + g)),
                  pl.BlockSpec((1, rb, 2 * hs), lambda b, g, r: (g, b * nrb + r, 0)),
                  pl.BlockSpec((1, ncb, hs, CHUNK), lambda b, g, r: (g, b * nrb + r, 0, 0)),
                  pl.BlockSpec((1, GDN_DV), lambda b, g, r: (0, 0))],
        out_specs=pl.BlockSpec((rb, vq), lambda b, g, r: (b * nrb + r, g)),
        out_shape=jax.ShapeDtypeStruct((m, GDN_V_DIM), BF16),
        scratch_shapes=[pltpu.VMEM((hs, GDN_DK, GDN_DV), F32)],
        compiler_params=_params(("arbitrary", "arbitrary", "arbitrary"), blocks,
                                _nbytes((hs, GDN_DK, GDN_DV), F32)),
        name="gdn_chunk",
    )(qkv, qkv, qkv, proj, col, row, norm_w.reshape(1, GDN_DV).astype(F32))


def _short_conv_mixer(h, hn, w_in, conv_w, w_out, t):
    bcu = _mm_plain(hn, w_in, BF16, t)
    y = _sconv_gate(bcu, conv_w, t)
    return _mm_residual(y, w_out, h, t)


def _gated_deltanet_mixer(h, hn, w_in, conv_w, a_log, dt_bias, norm_w, w_out, bsz, t):
    main_cols = GDN_CONV_DIM + GDN_V_DIM
    proj = _mm_plain(hn, w_in, BF16, t, n_cols=main_cols)
    w_ba = jnp.pad(w_in[:, main_cols:], ((0, 0), (0, LANES - 2 * GDN_NV)))
    ba = _mm_plain(hn, w_ba, F32, t)
    gates = _gdn_gates(ba, a_log, dt_bias, t)
    qkv = _gdn_conv(proj, conv_w, t, F32)
    og = _gdn_chunk(qkv, proj, gates, norm_w, bsz, t)
    return _mm_residual(og, w_out, h, t)


def _ffn(h, hn, w_gate, w_up, w_down, t):
    a = _mm_swiglu(hn, w_gate, w_up, t)
    return _mm_residual(a, w_down, h, t)


def kernel(x, meta_tokens, mixer_norm, ffn_norm, sc_w_in, sc_conv_w, sc_w_out, gdn_w_in, gdn_conv_w, gdn_a_log, gdn_dt_bias, gdn_norm_w, gdn_w_out, ffn_w_gate, ffn_w_up, ffn_w_down, final_norm):
    bsz, seq, d = x.shape
    depth = mixer_norm.shape[0]
    t = PAD_FRONT + N_META + seq
    assert t % CHUNK == 0
    meta = jnp.broadcast_to(meta_tokens.astype(x.dtype)[None], (bsz, N_META, d))
    h = jnp.concatenate([jnp.zeros((bsz, PAD_FRONT, d), x.dtype), meta, x], axis=1)
    h = h.reshape(bsz * t, d)
    for i in range(depth):
        j = i // 2
        hn = _rmsnorm(h, mixer_norm[i], BF16, t)
        if i % 2 == 0:
            h = _short_conv_mixer(h, hn, sc_w_in[j], sc_conv_w[j], sc_w_out[j], t)
        else:
            h = _gated_deltanet_mixer(h, hn, gdn_w_in[j], gdn_conv_w[j], gdn_a_log[j], gdn_dt_bias[j],
                                      gdn_norm_w[j], gdn_w_out[j], bsz, t)
        hn = _rmsnorm(h, ffn_norm[i], BF16, t)
        h = _ffn(h, hn, ffn_w_gate[i], ffn_w_up[i], ffn_w_down[i], t)
    out = _rmsnorm(h, final_norm, F32, t)
    return out.reshape(bsz, t, d)[:, PAD_FRONT + N_META:]
```

```python
import functools
import math

import jax
import jax.numpy as jnp
from jax import lax
from jax.experimental import pallas as pl
from jax.experimental.pallas import tpu as pltpu

F32 = jnp.float32
BF16 = jnp.bfloat16

N_META = 16
SC_WIDTH = 3
GDN_NK = 16
GDN_NV = 32
GDN_DK = 128
GDN_DV = 128
GDN_QK_DIM = GDN_NK * GDN_DK
GDN_V_DIM = GDN_NV * GDN_DV
GDN_CONV_DIM = 2 * GDN_QK_DIM + GDN_V_DIM
GDN_CONV_WIDTH = 4
CHUNK = 64
NORM_EPS = 1e-6
L2_EPS = 1e-6
PAD_FRONT = (-N_META) % CHUNK

LANES = 128
SUBLANES = 8
BF16_ROWS = 16
VMEM_BYTES_V7X = 64 * 1024 * 1024
VMEM_HEADROOM = 8 * 1024 * 1024

HEADS_PER_STEP = 4
KHEADS_PER_STEP = HEADS_PER_STEP * GDN_NK // GDN_NV
HALO = SUBLANES
GDN_CHUNKS_PER_STEP = 5


def _row_tile(t, target, mult=BF16_ROWS):
    best = None
    for d in range(mult, min(t, target) + 1, mult):
        if t % d == 0:
            best = d
    assert best is not None, (t, target, mult)
    return best


def _col_tile(n, target):
    best = None
    for d in range(LANES, min(n, target) + 1, LANES):
        if n % d == 0:
            best = d
    assert best is not None, (n, target)
    return best


def _nbytes(shape, dtype):
    return math.prod(shape) * jnp.dtype(dtype).itemsize


def _params(semantics, block_bytes, scratch_bytes=0):
    need = 2 * block_bytes + scratch_bytes + VMEM_HEADROOM
    return pltpu.CompilerParams(
        dimension_semantics=semantics,
        vmem_limit_bytes=min(max(need, 32 * 1024 * 1024), VMEM_BYTES_V7X - 4 * 1024 * 1024),
    )


def _rmsnorm_kernel(h_ref, w_ref, o_ref):
    x = h_ref[...]
    y = x * lax.rsqrt(jnp.mean(x * x, axis=-1, keepdims=True) + NORM_EPS)
    o_ref[...] = (y * w_ref[...]).astype(o_ref.dtype)


def _rmsnorm(h, w, out_dtype, t):
    m, d = h.shape
    tr = _row_tile(t, 512)
    blocks = _nbytes((tr, d), F32) + _nbytes((tr, d), out_dtype)
    return pl.pallas_call(
        _rmsnorm_kernel,
        grid=(m // tr,),
        in_specs=[pl.BlockSpec((tr, d), lambda i: (i, 0)),
                  pl.BlockSpec((1, d), lambda i: (0, 0))],
        out_specs=pl.BlockSpec((tr, d), lambda i: (i, 0)),
        out_shape=jax.ShapeDtypeStruct((m, d), out_dtype),
        compiler_params=_params(("arbitrary",), blocks),
        name="rmsnorm",
    )(h, w.reshape(1, d).astype(F32))


def _cast_weight(w_ref, wb_ref):
    @pl.when(pl.program_id(1) == 0)
    def _():
        wb_ref[...] = w_ref[...].astype(wb_ref.dtype)


def _mm_plain_kernel(x_ref, w_ref, o_ref, wb_ref):
    _cast_weight(w_ref, wb_ref)
    acc = jnp.dot(x_ref[...], wb_ref[...], preferred_element_type=F32)
    o_ref[...] = acc.astype(o_ref.dtype)


def _mm_residual_kernel(x_ref, w_ref, r_ref, o_ref, wb_ref):
    _cast_weight(w_ref, wb_ref)
    acc = jnp.dot(x_ref[...], wb_ref[...], preferred_element_type=F32)
    o_ref[...] = r_ref[...] + acc


def _mm_swiglu_kernel(x_ref, wg_ref, wu_ref, o_ref, wgb_ref, wub_ref):
    _cast_weight(wg_ref, wgb_ref)
    _cast_weight(wu_ref, wub_ref)
    x = x_ref[...]
    g = jnp.dot(x, wgb_ref[...], preferred_element_type=F32)
    u = jnp.dot(x, wub_ref[...], preferred_element_type=F32)
    o_ref[...] = (g * jax.nn.sigmoid(g) * u).astype(o_ref.dtype)


def _mm_tiles(t, k, n, n_weights):
    tn = _col_tile(n, 1024 if k <= 2048 and n_weights == 1 else 512)
    tm = _row_tile(t, 1040 if k <= 4096 else 640)
    return tm, tn


def _mm_plain(x, w, out_dtype, t, n_cols=None, col_block_offset=0):
    m, k = x.shape
    n = w.shape[1] if n_cols is None else n_cols
    tm, tn = _mm_tiles(t, k, n, 1)
    off = col_block_offset
    blocks = (_nbytes((tm, k), BF16) + _nbytes((k, tn), F32) + _nbytes((tm, tn), out_dtype))
    return pl.pallas_call(
        _mm_plain_kernel,
        grid=(n // tn, m // tm),
        in_specs=[pl.BlockSpec((tm, k), lambda j, i: (i, 0)),
                  pl.BlockSpec((k, tn), lambda j, i: (0, j + off))],
        out_specs=pl.BlockSpec((tm, tn), lambda j, i: (i, j)),
        out_shape=jax.ShapeDtypeStruct((m, n), out_dtype),
        scratch_shapes=[pltpu.VMEM((k, tn), BF16)],
        compiler_params=_params(("arbitrary", "arbitrary"), blocks, _nbytes((k, tn), BF16)),
        name="mm_plain",
    )(x, w)


def _mm_residual(x, w, res, t):
    m, k = x.shape
    n = w.shape[1]
    tm, tn = _mm_tiles(t, k, n, 1)
    blocks = (_nbytes((tm, k), BF16) + _nbytes((k, tn), F32) + 2 * _nbytes((tm, tn), F32))
    return pl.pallas_call(
        _mm_residual_kernel,
        grid=(n // tn, m // tm),
        in_specs=[pl.BlockSpec((tm, k), lambda j, i: (i, 0)),
                  pl.BlockSpec((k, tn), lambda j, i: (0, j)),
                  pl.BlockSpec((tm, tn), lambda j, i: (i, j))],
        out_specs=pl.BlockSpec((tm, tn), lambda j, i: (i, j)),
        out_shape=jax.ShapeDtypeStruct((m, n), F32),
        scratch_shapes=[pltpu.VMEM((k, tn), BF16)],
        compiler_params=_params(("arbitrary", "arbitrary"), blocks, _nbytes((k, tn), BF16)),
        name="mm_residual",
    )(x, w, res)


def _mm_swiglu(x, wg, wu, t):
    m, k = x.shape
    n = wg.shape[1]
    tm, tn = _mm_tiles(t, k, n, 2)
    blocks = (_nbytes((tm, k), BF16) + 2 * _nbytes((k, tn), F32) + _nbytes((tm, tn), BF16))
    return pl.pallas_call(
        _mm_swiglu_kernel,
        grid=(n // tn, m // tm),
        in_specs=[pl.BlockSpec((tm, k), lambda j, i: (i, 0)),
                  pl.BlockSpec((k, tn), lambda j, i: (0, j)),
                  pl.BlockSpec((k, tn), lambda j, i: (0, j))],
        out_specs=pl.BlockSpec((tm, tn), lambda j, i: (i, j)),
        out_shape=jax.ShapeDtypeStruct((m, n), BF16),
        scratch_shapes=[pltpu.VMEM((k, tn), BF16), pltpu.VMEM((k, tn), BF16)],
        compiler_params=_params(("arbitrary", "arbitrary"), blocks, 2 * _nbytes((k, tn), BF16)),
        name="mm_swiglu",
    )(x, wg, wu)


def _causal_conv(x, carry_ref, w, first_tile):
    @pl.when(first_tile)
    def _():
        carry_ref[...] = jnp.zeros_like(carry_ref)

    r = x.shape[0]
    width = w.shape[0]
    xe = jnp.concatenate([carry_ref[...], x], axis=0)
    y = w[width - 1:width] * x
    for s in range(1, width):
        y = y + w[width - 1 - s:width - s] * xe[HALO - s:HALO - s + r]
    carry_ref[...] = x[r - HALO:]
    return y


def _sconv_kernel(b_ref, c_ref, u_ref, w_ref, o_ref, carry_ref, *, tiles_per_seq):
    cu = c_ref[...].astype(F32) * u_ref[...].astype(F32)
    y = _causal_conv(cu, carry_ref, w_ref[...], pl.program_id(1) % tiles_per_seq == 0)
    o_ref[...] = (b_ref[...].astype(F32) * y).astype(o_ref.dtype)


def _sconv_gate(bcu, conv_w, t):
    m = bcu.shape[0]
    d = bcu.shape[1] // 3
    tr = _row_tile(t, 1040)
    tc = _col_tile(d, 512)
    nb = d // tc
    blocks = 4 * _nbytes((tr, tc), BF16) + _nbytes((SC_WIDTH, tc), F32)
    return pl.pallas_call(
        functools.partial(_sconv_kernel, tiles_per_seq=t // tr),
        grid=(nb, m // tr),
        in_specs=[pl.BlockSpec((tr, tc), lambda j, i: (i, j)),
                  pl.BlockSpec((tr, tc), lambda j, i: (i, j + nb)),
                  pl.BlockSpec((tr, tc), lambda j, i: (i, j + 2 * nb)),
                  pl.BlockSpec((SC_WIDTH, tc), lambda j, i: (0, j))],
        out_specs=pl.BlockSpec((tr, tc), lambda j, i: (i, j)),
        out_shape=jax.ShapeDtypeStruct((m, d), BF16),
        scratch_shapes=[pltpu.VMEM((HALO, tc), F32)],
        compiler_params=_params(("arbitrary", "arbitrary"), blocks, 16 * _nbytes((tr, tc), F32)),
        name="sconv_gate",
    )(bcu, bcu, bcu, conv_w.astype(F32))


def _gdn_conv_kernel(x_ref, w_ref, o_ref, carry_ref, *, tiles_per_seq, n_l2_blocks, n_q_blocks):
    j = pl.program_id(0)
    x = x_ref[...].astype(F32)
    y = _causal_conv(x, carry_ref, w_ref[...], pl.program_id(1) % tiles_per_seq == 0)
    y = y * jax.nn.sigmoid(y)

    @pl.when(j < n_l2_blocks)
    def _():
        scale = jnp.where(j < n_q_blocks, GDN_DK ** -0.5, 1.0).astype(F32)
        for hh in range(y.shape[1] // GDN_DK):
            ys = y[:, hh * GDN_DK:(hh + 1) * GDN_DK]
            inv = lax.rsqrt(jnp.sum(ys * ys, axis=-1, keepdims=True) + L2_EPS) * scale
            o_ref[:, hh * GDN_DK:(hh + 1) * GDN_DK] = (ys * inv).astype(o_ref.dtype)

    @pl.when(j >= n_l2_blocks)
    def _():
        o_ref[...] = y.astype(o_ref.dtype)


def _gdn_conv(proj, conv_w, t, out_dtype):
    m = proj.shape[0]
    tr = _row_tile(t, 1040)
    tc = 512
    blocks = _nbytes((tr, tc), proj.dtype) + _nbytes((tr, tc), out_dtype) + _nbytes((GDN_CONV_WIDTH, tc), F32)
    return pl.pallas_call(
        functools.partial(_gdn_conv_kernel, tiles_per_seq=t // tr,
                          n_l2_blocks=2 * GDN_QK_DIM // tc, n_q_blocks=GDN_QK_DIM // tc),
        grid=(GDN_CONV_DIM // tc, m // tr),
        in_specs=[pl.BlockSpec((tr, tc), lambda j, i: (i, j)),
                  pl.BlockSpec((GDN_CONV_WIDTH, tc), lambda j, i: (0, j))],
        out_specs=pl.BlockSpec((tr, tc), lambda j, i: (i, j)),
        out_shape=jax.ShapeDtypeStruct((m, GDN_CONV_DIM), out_dtype),
        scratch_shapes=[pltpu.VMEM((HALO, tc), F32)],
        compiler_params=_params(("arbitrary", "arbitrary"), blocks, 16 * _nbytes((tr, tc), F32)),
        name="gdn_conv",
    )(proj, conv_w.astype(F32))


def _gdn_gates_kernel(ba_ref, alog_ref, dtb_ref, o_ref):
    x = ba_ref[...]
    r = x.shape[0]
    z = x + dtb_ref[...]
    softplus = jnp.maximum(z, 0.0) + jnp.log1p(jnp.exp(-jnp.abs(z)))
    g = -jnp.exp(alog_ref[...]) * softplus
    pos = lax.broadcasted_iota(jnp.int32, x.shape, 0) % CHUNK
    s = 1
    while s < CHUNK:
        g = g + jnp.where(pos >= s, pltpu.roll(g, s, 0), 0.0)
        s *= 2
    lane = lax.broadcasted_iota(jnp.int32, x.shape, 1)
    o_ref[...] = jnp.where(lane < GDN_NV, jax.nn.sigmoid(x), g)


def _gdn_gates(ba, a_log, dt_bias, t):
    m = ba.shape[0]
    tr = _row_tile(t, 1040, CHUNK)
    pad = LANES - 2 * GDN_NV
    alog = jnp.concatenate([jnp.zeros((GDN_NV,), F32), a_log.astype(F32), jnp.zeros((pad,), F32)])
    dtb = jnp.concatenate([jnp.zeros((GDN_NV,), F32), dt_bias.astype(F32), jnp.zeros((pad,), F32)])
    return pl.pallas_call(
        _gdn_gates_kernel,
        grid=(m // tr,),
        in_specs=[pl.BlockSpec((tr, LANES), lambda i: (i, 0)),
                  pl.BlockSpec((1, LANES), lambda i: (0, 0)),
                  pl.BlockSpec((1, LANES), lambda i: (0, 0))],
        out_specs=pl.BlockSpec((tr, LANES), lambda i: (i, 0)),
        out_shape=jax.ShapeDtypeStruct((m, LANES), F32),
        compiler_params=_params(("arbitrary",), 2 * _nbytes((tr, LANES), F32)),
        name="gdn_gates",
    )(ba, alog.reshape(1, LANES), dtb.reshape(1, LANES))


def _bdot(a, b):
    return jnp.dot(a.astype(BF16), b.astype(BF16), preferred_element_type=F32)


def _gdn_chunk_kernel(q_ref, k_ref, v_ref, z_ref, col_ref, row_ref, nw_ref, o_ref, s_ref):
    n_chunks = q_ref.shape[0] // CHUNK
    n_heads = HEADS_PER_STEP
    rep = n_heads // KHEADS_PER_STEP
    chunks = range(n_chunks)
    heads = range(n_heads)
    pairs = [(c, h) for c in chunks for h in heads]

    @pl.when(pl.program_id(2) == 0)
    def _():
        s_ref[...] = jnp.zeros_like(s_ref)

    ci = lax.broadcasted_iota(jnp.int32, (CHUNK, CHUNK), 0)
    si = lax.broadcasted_iota(jnp.int32, (CHUNK, CHUNK), 1)
    causal = ci >= si
    strict = ci > si
    blk16 = (ci >> 4) == (si >> 4)
    blk32 = (ci >> 5) == (si >> 5)
    eye = (ci == si).astype(F32)
    upper_half = lax.broadcasted_iota(jnp.int32, (CHUNK, 2 * CHUNK), 1) >= CHUNK
    nw = nw_ref[...]

    def rows(c):
        return slice(c * CHUNK, (c + 1) * CHUNK)

    def hcols(h, width):
        return slice(h * width, (h + 1) * width)

    qc = {(c, kh): q_ref[rows(c), hcols(kh, GDN_DK)].astype(F32)
          for c in chunks for kh in range(KHEADS_PER_STEP)}
    kc = {(c, kh): k_ref[rows(c), hcols(kh, GDN_DK)].astype(F32)
          for c in chunks for kh in range(KHEADS_PER_STEP)}
    beta = {(c, h): col_ref[0, rows(c), h:h + 1] for c, h in pairs}
    gcol = {(c, h): col_ref[0, rows(c), n_heads + h:n_heads + h + 1] for c, h in pairs}
    grow = {(c, h): row_ref[0, c, h:h + 1, :] for c, h in pairs}
    glast = {p: gcol[p][CHUNK - 1:CHUNK, :] for p in pairs}
    eg = {p: jnp.exp(gcol[p]) for p in pairs}
    decay = {p: jnp.where(causal, jnp.exp(gcol[p] - grow[p]), 0.0) for p in pairs}
    kb = {(c, h): kc[c, h // rep] * beta[c, h] for c, h in pairs}

    kk = {}
    for c in chunks:
        for kh in range(KHEADS_PER_STEP):
            lhs = jnp.concatenate([kb[c, kh * rep + r] for r in range(rep)] + [qc[c, kh]], axis=0)
            kk[c, kh] = lax.dot_general(lhs.astype(BF16), kc[c, kh].astype(BF16),
                                        (((1,), (1,)), ((), ())), preferred_element_type=F32)

    nmat = {(c, h): jnp.where(strict, -kk[c, h // rep][(h % rep) * CHUNK:(h % rep + 1) * CHUNK]
                              * decay[c, h], 0.0) for c, h in pairs}
    qk = {(c, h): jnp.where(causal, kk[c, h // rep][rep * CHUNK:] * decay[c, h], 0.0).astype(BF16)
          for c, h in pairs}
    nd = {p: jnp.where(blk16, nmat[p], 0.0) for p in pairs}
    no1 = {p: (jnp.where(blk32, nmat[p], 0.0) - nd[p]).astype(BF16) for p in pairs}
    no2 = {p: jnp.where(blk32, 0.0, nmat[p]).astype(BF16) for p in pairs}
    p1 = {p: _bdot(nd[p], nd[p]) for p in pairs}
    ps = {p: jnp.concatenate([p1[p], eye + nd[p]], axis=1) for p in pairs}
    for _ in range(2):
        ps = {p: _bdot(ps[p][:, :CHUNK], ps[p]) + jnp.where(upper_half, ps[p], 0.0) for p in pairs}
    t16 = {p: (_bdot(ps[p][:, :CHUNK], ps[p]) + ps[p])[:, CHUNK:] for p in pairs}
    x1 = {p: jnp.dot(no1[p], t16[p].astype(BF16), preferred_element_type=F32) for p in pairs}
    tm = {p: (t16[p] + _bdot(t16[p], x1[p]) - eye).astype(BF16) for p in pairs}
    rhs = {(c, h): jnp.concatenate([v_ref[rows(c), hcols(h, GDN_DV)].astype(F32) * beta[c, h],
                                    kb[c, h] * eg[c, h]], axis=1) for c, h in pairs}
    y = {p: rhs[p] + jnp.dot(tm[p], rhs[p].astype(BF16), preferred_element_type=F32) for p in pairs}
    zz = {p: jnp.dot(no2[p], y[p].astype(BF16), preferred_element_type=F32) for p in pairs}
    uw = {p: y[p] + zz[p] + jnp.dot(tm[p], zz[p].astype(BF16), preferred_element_type=F32)
          for p in pairs}
    wq = {(c, h): jnp.concatenate([uw[c, h][:, GDN_DV:], qc[c, h // rep] * eg[c, h]],
                                  axis=0).astype(BF16) for c, h in pairs}
    ktail = {(c, h): (kc[c, h // rep] * jnp.exp(glast[c, h] - gcol[c, h])).astype(BF16)
             for c, h in pairs}

    state = [s_ref[h] for h in heads]
    for c in chunks:
        ws_qs = [jnp.dot(wq[c, h], state[h].astype(BF16), preferred_element_type=F32) for h in heads]
        v_new = [(uw[c, h][:, :GDN_DV] - ws_qs[h][:CHUNK]).astype(BF16) for h in heads]
        o2 = [jnp.dot(qk[c, h], v_new[h], preferred_element_type=F32) for h in heads]
        ds = [lax.dot_general(ktail[c, h], v_new[h], (((0,), (0,)), ((), ())),
                              preferred_element_type=F32) for h in heads]
        state = [state[h] * jnp.exp(glast[c, h]) + ds[h] for h in heads]
        for h in heads:
            o = ws_qs[h][CHUNK:] + o2[h]
            zc = z_ref[rows(c), hcols(h, GDN_DV)].astype(F32)
            on = o * lax.rsqrt(jnp.mean(o * o, axis=-1, keepdims=True) + NORM_EPS) * nw
            o_ref[rows(c), hcols(h, GDN_DV)] = (on * (zc * jax.nn.sigmoid(zc))).astype(o_ref.dtype)
    for h in heads:
        s_ref[h] = state[h]


def _gdn_chunk(qkv, proj, gates, norm_w, bsz, t):
    m = qkv.shape[0]
    hs, khs = HEADS_PER_STEP, KHEADS_PER_STEP
    n_groups = GDN_NV // hs
    rb = _row_tile(t, GDN_CHUNKS_PER_STEP * CHUNK, CHUNK)
    nrb = t // rb
    ncb = rb // CHUNK
    n_chunks_total = m // CHUNK
    beta = gates[:, :GDN_NV].reshape(m, n_groups, hs)
    gc = gates[:, GDN_NV:2 * GDN_NV].reshape(m, n_groups, hs)
    col = jnp.transpose(jnp.concatenate([beta, gc], axis=-1), (1, 0, 2))
    row = jnp.transpose(gc.reshape(n_chunks_total, CHUNK, n_groups, hs), (2, 0, 3, 1))

    kq = khs * GDN_DK
    vq = hs * GDN_DV
    q_off = 0
    k_off = GDN_QK_DIM // kq
    v_off = 2 * GDN_QK_DIM // vq
    z_off = GDN_CONV_DIM // vq
    blocks = (2 * _nbytes((rb, kq), qkv.dtype) + _nbytes((rb, vq), qkv.dtype)
              + _nbytes((rb, vq), proj.dtype) + _nbytes((rb, LANES), F32)
              + _nbytes((ncb, SUBLANES, LANES), F32) + _nbytes((rb, vq), BF16))
    return pl.pallas_call(
        _gdn_chunk_kernel,
        grid=(bsz, n_groups, nrb),
        in_specs=[pl.BlockSpec((rb, kq), lambda b, g, r: (b * nrb + r, q_off + g)),
                  pl.BlockSpec((rb, kq), lambda b, g, r: (b * nrb + r, k_off + g)),
                  pl.BlockSpec((rb, vq), lambda b, g, r: (b * nrb + r, v_off + g)),
                  pl.BlockSpec((rb, vq), lambda b, g, r: (b * nrb + r, z_off + g)),
                  pl.BlockSpec((1, rb, 2 * hs), lambda b, g, r: (g, b * nrb + r, 0)),
                  pl.BlockSpec((1, ncb, hs, CHUNK), lambda b, g, r: (g, b * nrb + r, 0, 0)),
                  pl.BlockSpec((1, GDN_DV), lambda b, g, r: (0, 0))],
        out_specs=pl.BlockSpec((rb, vq), lambda b, g, r: (b * nrb + r, g)),
        out_shape=jax.ShapeDtypeStruct((m, GDN_V_DIM), BF16),
        scratch_shapes=[pltpu.VMEM((hs, GDN_DK, GDN_DV), F32)],
        compiler_params=_params(("arbitrary", "arbitrary", "arbitrary"), blocks,
                                _nbytes((hs, GDN_DK, GDN_DV), F32)),
        name="gdn_chunk",
    )(qkv, qkv, qkv, proj, col, row, norm_w.reshape(1, GDN_DV).astype(F32))


def _short_conv_mixer(h, hn, w_in, conv_w, w_out, t):
    bcu = _mm_plain(hn, w_in, BF16, t)
    y = _sconv_gate(bcu, conv_w, t)
    return _mm_residual(y, w_out, h, t)


def _gated_deltanet_mixer(h, hn, w_in, conv_w, a_log, dt_bias, norm_w, w_out, bsz, t):
    main_cols = GDN_CONV_DIM + GDN_V_DIM
    proj = _mm_plain(hn, w_in, BF16, t, n_cols=main_cols)
    w_ba = jnp.pad(w_in[:, main_cols:], ((0, 0), (0, LANES - 2 * GDN_NV)))
    ba = _mm_plain(hn, w_ba, F32, t)
    gates = _gdn_gates(ba, a_log, dt_bias, t)
    qkv = _gdn_conv(proj, conv_w, t, F32)
    og = _gdn_chunk(qkv, proj, gates, norm_w, bsz, t)
    return _mm_residual(og, w_out, h, t)


def _ffn(h, hn, w_gate, w_up, w_down, t):
    a = _mm_swiglu(hn, w_gate, w_up, t)
    return _mm_residual(a, w_down, h, t)


def kernel(x, meta_tokens, mixer_norm, ffn_norm, sc_w_in, sc_conv_w, sc_w_out, gdn_w_in, gdn_conv_w, gdn_a_log, gdn_dt_bias, gdn_norm_w, gdn_w_out, ffn_w_gate, ffn_w_up, ffn_w_down, final_norm):
    bsz, seq, d = x.shape
    depth = mixer_norm.shape[0]
    t = PAD_FRONT + N_META + seq
    assert t % CHUNK == 0
    meta = jnp.broadcast_to(meta_tokens.astype(x.dtype)[None], (bsz, N_META, d))
    h = jnp.concatenate([jnp.zeros((bsz, PAD_FRONT, d), x.dtype), meta, x], axis=1)
    h = h.reshape(bsz * t, d)
    for i in range(depth):
        j = i // 2
        hn = _rmsnorm(h, mixer_norm[i], BF16, t)
        if i % 2 == 0:
            h = _short_conv_mixer(h, hn, sc_w_in[j], sc_conv_w[j], sc_w_out[j], t)
        else:
            h = _gated_deltanet_mixer(h, hn, gdn_w_in[j], gdn_conv_w[j], gdn_a_log[j], gdn_dt_bias[j],
                                      gdn_norm_w[j], gdn_w_out[j], bsz, t)
        hn = _rmsnorm(h, ffn_norm[i], BF16, t)
        h = _ffn(h, hn, ffn_w_gate[i], ffn_w_up[i], ffn_w_down[i], t)
    out = _rmsnorm(h, final_norm, F32, t)
    return out.reshape(bsz, t, d)[:, PAD_FRONT + N_META:]
```

```python
import functools
import math

import jax
import jax.numpy as jnp
from jax import lax
from jax.experimental import pallas as pl
from jax.experimental.pallas import tpu as pltpu

F32 = jnp.float32
BF16 = jnp.bfloat16

N_META = 16
SC_WIDTH = 3
GDN_NK = 16
GDN_NV = 32
GDN_DK = 128
GDN_DV = 128
GDN_QK_DIM = GDN_NK * GDN_DK
GDN_V_DIM = GDN_NV * GDN_DV
GDN_CONV_DIM = 2 * GDN_QK_DIM + GDN_V_DIM
GDN_CONV_WIDTH = 4
CHUNK = 64
NORM_EPS = 1e-6
L2_EPS = 1e-6
PAD_FRONT = (-N_META) % CHUNK

LANES = 128
SUBLANES = 8
BF16_ROWS = 16
VMEM_BYTES_V7X = 64 * 1024 * 1024
VMEM_HEADROOM = 8 * 1024 * 1024

HEADS_PER_STEP = 4
KHEADS_PER_STEP = HEADS_PER_STEP * GDN_NK // GDN_NV
HALO = SUBLANES
GDN_CHUNKS_PER_STEP = 5
CONV_SUB = BF16_ROWS
CONV_UNROLL = 13
CONV_ROW_TILE = 4160
CONV_COL_TILE = 2 * LANES


def _row_tile(t, target, mult=BF16_ROWS):
    best = None
    for d in range(mult, min(t, target) + 1, mult):
        if t % d == 0:
            best = d
    assert best is not None, (t, target, mult)
    return best


def _col_tile(n, target):
    best = None
    for d in range(LANES, min(n, target) + 1, LANES):
        if n % d == 0:
            best = d
    assert best is not None, (n, target)
    return best


def _nbytes(shape, dtype):
    return math.prod(shape) * jnp.dtype(dtype).itemsize


def _params(semantics, block_bytes, scratch_bytes=0):
    need = 2 * block_bytes + scratch_bytes + VMEM_HEADROOM
    return pltpu.CompilerParams(
        dimension_semantics=semantics,
        vmem_limit_bytes=min(max(need, 32 * 1024 * 1024), VMEM_BYTES_V7X - 4 * 1024 * 1024),
    )


def _rmsnorm_kernel(h_ref, w_ref, o_ref):
    x = h_ref[...]
    y = x * lax.rsqrt(jnp.mean(x * x, axis=-1, keepdims=True) + NORM_EPS)
    o_ref[...] = (y * w_ref[...]).astype(o_ref.dtype)


def _rmsnorm(h, w, out_dtype, t):
    m, d = h.shape
    tr = _row_tile(t, 512)
    blocks = _nbytes((tr, d), F32) + _nbytes((tr, d), out_dtype)
    return pl.pallas_call(
        _rmsnorm_kernel,
        grid=(m // tr,),
        in_specs=[pl.BlockSpec((tr, d), lambda i: (i, 0)),
                  pl.BlockSpec((1, d), lambda i: (0, 0))],
        out_specs=pl.BlockSpec((tr, d), lambda i: (i, 0)),
        out_shape=jax.ShapeDtypeStruct((m, d), out_dtype),
        compiler_params=_params(("arbitrary",), blocks),
        name="rmsnorm",
    )(h, w.reshape(1, d).astype(F32))


def _embed_kernel(x_ref, meta_ref, w_ref, h_ref, hn_ref, *, skip):
    def finish(rows):
        h_ref[...] = rows
        y = rows * lax.rsqrt(jnp.mean(rows * rows, axis=-1, keepdims=True) + NORM_EPS)
        hn_ref[...] = (y * w_ref[...]).astype(hn_ref.dtype)

    tr, d = h_ref.shape

    @pl.when(pl.program_id(1) == 0)
    def _():
        finish(jnp.concatenate([jnp.zeros((skip - N_META, d), F32), meta_ref[...],
                                x_ref[:tr - skip, :]], axis=0))

    @pl.when(pl.program_id(1) != 0)
    def _():
        finish(x_ref[...])


def _embed(x, meta_tokens, w, t):
    bsz, seq, d = x.shape
    skip = t - seq
    tr = _row_tile(t, 1040)
    tiles = t // tr
    xr = tr if tiles > 1 else seq
    assert skip < tr and xr <= seq
    blocks = _nbytes((xr, d), F32) + _nbytes((tr, d), F32) + _nbytes((tr, d), BF16)
    return pl.pallas_call(
        functools.partial(_embed_kernel, skip=skip),
        grid=(bsz, tiles),
        in_specs=[pl.BlockSpec((pl.Element(xr), pl.Element(d)),
                               lambda b, k: (pl.multiple_of(
                                   b * seq + jnp.maximum(k * tr - skip, 0), SUBLANES), 0)),
                  pl.BlockSpec((N_META, d), lambda b, k: (0, 0)),
                  pl.BlockSpec((1, d), lambda b, k: (0, 0))],
        out_specs=[pl.BlockSpec((tr, d), lambda b, k: (b * tiles + k, 0)),
                   pl.BlockSpec((tr, d), lambda b, k: (b * tiles + k, 0))],
        out_shape=[jax.ShapeDtypeStruct((bsz * t, d), F32),
                   jax.ShapeDtypeStruct((bsz * t, d), BF16)],
        compiler_params=_params(("arbitrary", "arbitrary"), blocks),
        name="embed",
    )(x.reshape(bsz * seq, d), meta_tokens.astype(F32), w.reshape(1, d).astype(F32))


def _final_rmsnorm(h, w, bsz, t, seq):
    d = h.shape[1]
    tr = _row_tile(seq, 512)
    nt = seq // tr
    skip = t - seq
    blocks = 2 * _nbytes((tr, d), F32)
    out = pl.pallas_call(
        _rmsnorm_kernel,
        grid=(bsz, nt),
        in_specs=[pl.BlockSpec((pl.Element(tr), pl.Element(d)),
                               lambda b, i: (pl.multiple_of(b * t + skip + i * tr, SUBLANES), 0)),
                  pl.BlockSpec((1, d), lambda b, i: (0, 0))],
        out_specs=pl.BlockSpec((tr, d), lambda b, i: (b * nt + i, 0)),
        out_shape=jax.ShapeDtypeStruct((bsz * seq, d), F32),
        compiler_params=_params(("arbitrary", "arbitrary"), blocks),
        name="final_rmsnorm",
    )(h, w.reshape(1, d).astype(F32))
    return out.reshape(bsz, seq, d)


def _cast_weight(w_ref, wb_ref):
    @pl.when(pl.program_id(1) == 0)
    def _():
        wb_ref[...] = w_ref[...].astype(wb_ref.dtype)


def _mm_plain_kernel(x_ref, w_ref, o_ref, wb_ref):
    _cast_weight(w_ref, wb_ref)
    acc = jnp.dot(x_ref[...], wb_ref[...], preferred_element_type=F32)
    o_ref[...] = acc.astype(o_ref.dtype)


def _mm_residual_kernel(x_ref, w_ref, r_ref, o_ref, wb_ref):
    _cast_weight(w_ref, wb_ref)
    acc = jnp.dot(x_ref[...], wb_ref[...], preferred_element_type=F32)
    o_ref[...] = r_ref[...] + acc


def _mm_swiglu_kernel(x_ref, wg_ref, wu_ref, o_ref, wgb_ref, wub_ref):
    _cast_weight(wg_ref, wgb_ref)
    _cast_weight(wu_ref, wub_ref)
    x = x_ref[...]
    g = jnp.dot(x, wgb_ref[...], preferred_element_type=F32)
    u = jnp.dot(x, wub_ref[...], preferred_element_type=F32)
    o_ref[...] = (g * jax.nn.sigmoid(g) * u).astype(o_ref.dtype)


_NT_DIMS = (((1,), (1,)), ((), ()))


def _mm_plain_t_kernel(x_ref, wt_ref, o_ref, wb_ref):
    _cast_weight(wt_ref, wb_ref)
    acc = lax.dot_general(x_ref[...], wb_ref[...], _NT_DIMS, preferred_element_type=F32)
    o_ref[...] = acc.astype(o_ref.dtype)


def _mm_tail_t_kernel(x_ref, wt_ref, o_ref, wb_ref):
    @pl.when(pl.program_id(0) == 0)
    def _():
        wb_ref[...] = jnp.zeros_like(wb_ref)
        wb_ref[:wt_ref.shape[0], :] = wt_ref[...].astype(wb_ref.dtype)

    o_ref[...] = lax.dot_general(x_ref[...], wb_ref[...], _NT_DIMS, preferred_element_type=F32)


def _mm_tiles(t, k, n, n_weights):
    tn = _col_tile(n, 1024 if k <= 2048 and n_weights == 1 else 512)
    tm = _row_tile(t, 1040 if k <= 4096 else 640)
    return tm, tn


def _mm_plain(x, w, layer, out_dtype, t, n_cols=None):
    m, k = x.shape
    n = w.shape[2] if n_cols is None else n_cols
    tm, tn = _mm_tiles(t, k, n, 1)
    blocks = (_nbytes((tm, k), BF16) + _nbytes((k, tn), F32) + _nbytes((tm, tn), out_dtype))
    return pl.pallas_call(
        _mm_plain_kernel,
        grid=(n // tn, m // tm),
        in_specs=[pl.BlockSpec((tm, k), lambda j, i: (i, 0)),
                  pl.BlockSpec((None, k, tn), lambda j, i: (layer, 0, j))],
        out_specs=pl.BlockSpec((tm, tn), lambda j, i: (i, j)),
        out_shape=jax.ShapeDtypeStruct((m, n), out_dtype),
        scratch_shapes=[pltpu.VMEM((k, tn), BF16)],
        compiler_params=_params(("arbitrary", "arbitrary"), blocks, _nbytes((k, tn), BF16)),
        name="mm_plain",
    )(x, w)


def _mm_plain_t(x, wt, layer, out_dtype, t, n_cols):
    m, k = x.shape
    tm, tn = _mm_tiles(t, k, n_cols, 1)
    blocks = (_nbytes((tm, k), BF16) + _nbytes((tn, k), F32) + _nbytes((tm, tn), out_dtype))
    return pl.pallas_call(
        _mm_plain_t_kernel,
        grid=(n_cols // tn, m // tm),
        in_specs=[pl.BlockSpec((tm, k), lambda j, i: (i, 0)),
                  pl.BlockSpec((None, tn, k), lambda j, i: (layer, j, 0))],
        out_specs=pl.BlockSpec((tm, tn), lambda j, i: (i, j)),
        out_shape=jax.ShapeDtypeStruct((m, n_cols), out_dtype),
        scratch_shapes=[pltpu.VMEM((tn, k), BF16)],
        compiler_params=_params(("arbitrary", "arbitrary"), blocks, _nbytes((tn, k), BF16)),
        name="mm_plain_t",
    )(x, wt)


def _mm_tail_t(x, wt, layer, row_start, t):
    m, k = x.shape
    valid = wt.shape[1] - row_start
    assert 0 < valid <= LANES and row_start % valid == 0 and valid % BF16_ROWS == 0
    tm = _row_tile(t, 1040)
    blocks = _nbytes((tm, k), BF16) + _nbytes((valid, k), F32) + _nbytes((tm, LANES), F32)
    return pl.pallas_call(
        _mm_tail_t_kernel,
        grid=(m // tm,),
        in_specs=[pl.BlockSpec((tm, k), lambda i: (i, 0)),
                  pl.BlockSpec((None, valid, k), lambda i: (layer, row_start // valid, 0))],
        out_specs=pl.BlockSpec((tm, LANES), lambda i: (i, 0)),
        out_shape=jax.ShapeDtypeStruct((m, LANES), F32),
        scratch_shapes=[pltpu.VMEM((LANES, k), BF16)],
        compiler_params=_params(("arbitrary",), blocks, _nbytes((LANES, k), BF16)),
        name="mm_tail_t",
    )(x, wt)


def _mm_residual(x, w, layer, res, t):
    m, k = x.shape
    n = w.shape[2]
    tm, tn = _mm_tiles(t, k, n, 1)
    blocks = (_nbytes((tm, k), BF16) + _nbytes((k, tn), F32) + 2 * _nbytes((tm, tn), F32))
    return pl.pallas_call(
        _mm_residual_kernel,
        grid=(n // tn, m // tm),
        in_specs=[pl.BlockSpec((tm, k), lambda j, i: (i, 0)),
                  pl.BlockSpec((None, k, tn), lambda j, i: (layer, 0, j)),
                  pl.BlockSpec((tm, tn), lambda j, i: (i, j))],
        out_specs=pl.BlockSpec((tm, tn), lambda j, i: (i, j)),
        out_shape=jax.ShapeDtypeStruct((m, n), F32),
        scratch_shapes=[pltpu.VMEM((k, tn), BF16)],
        compiler_params=_params(("arbitrary", "arbitrary"), blocks, _nbytes((k, tn), BF16)),
        name="mm_residual",
    )(x, w, res)


def _mm_swiglu(x, wg, wu, layer, t):
    m, k = x.shape
    n = wg.shape[2]
    tm, tn = _mm_tiles(t, k, n, 2)
    blocks = (_nbytes((tm, k), BF16) + 2 * _nbytes((k, tn), F32) + _nbytes((tm, tn), BF16))
    return pl.pallas_call(
        _mm_swiglu_kernel,
        grid=(n // tn, m // tm),
        in_specs=[pl.BlockSpec((tm, k), lambda j, i: (i, 0)),
                  pl.BlockSpec((None, k, tn), lambda j, i: (layer, 0, j)),
                  pl.BlockSpec((None, k, tn), lambda j, i: (layer, 0, j))],
        out_specs=pl.BlockSpec((tm, tn), lambda j, i: (i, j)),
        out_shape=jax.ShapeDtypeStruct((m, n), BF16),
        scratch_shapes=[pltpu.VMEM((k, tn), BF16), pltpu.VMEM((k, tn), BF16)],
        compiler_params=_params(("arbitrary", "arbitrary"), blocks, 2 * _nbytes((k, tn), BF16)),
        name="mm_swiglu",
    )(x, wg, wu)


def _conv_rows(load_rows, w_ref, carry_ref, n_rows, first_tile, emit):
    @pl.when(first_tile)
    def _():
        carry_ref[...] = jnp.zeros_like(carry_ref)

    width = w_ref.shape[0]
    n_sub = n_rows // CONV_SUB
    unroll = max(u for u in range(1, CONV_UNROLL + 1) if n_sub % u == 0)
    taps = [jnp.broadcast_to(w_ref[j:j + 1, :], (CONV_SUB, w_ref.shape[1])) for j in range(width)]

    def body(i, prev):
        for k in range(unroll):
            r0 = pl.multiple_of((i * unroll + k) * CONV_SUB, CONV_SUB)
            cur = load_rows(r0)
            xe = jnp.concatenate([prev, cur], axis=0)
            y = taps[width - 1] * cur
            for s in range(1, width):
                y = y + taps[width - 1 - s] * xe[HALO - s:HALO - s + CONV_SUB]
            emit(r0, y)
            prev = cur[CONV_SUB - HALO:]
        return prev

    carry_ref[...] = lax.fori_loop(0, n_sub // unroll, body, carry_ref[...])


def _sconv_kernel(b_ref, c_ref, u_ref, w_ref, o_ref, carry_ref, *, tiles_per_seq):
    def load_rows(r0):
        rows = pl.ds(r0, CONV_SUB)
        return c_ref[rows, :].astype(F32) * u_ref[rows, :].astype(F32)

    def emit(r0, y):
        rows = pl.ds(r0, CONV_SUB)
        o_ref[rows, :] = (b_ref[rows, :].astype(F32) * y).astype(o_ref.dtype)

    _conv_rows(load_rows, w_ref, carry_ref, o_ref.shape[0],
               pl.program_id(1) % tiles_per_seq == 0, emit)


def _sconv_gate(bcu, conv_w, t):
    m = bcu.shape[0]
    d = bcu.shape[1] // 3
    tr = _row_tile(t, CONV_ROW_TILE)
    tc = CONV_COL_TILE
    nb = d // tc
    blocks = 4 * _nbytes((tr, tc), BF16) + _nbytes((SC_WIDTH, tc), F32)
    return pl.pallas_call(
        functools.partial(_sconv_kernel, tiles_per_seq=t // tr),
        grid=(nb, m // tr),
        in_specs=[pl.BlockSpec((tr, tc), lambda j, i: (i, j)),
                  pl.BlockSpec((tr, tc), lambda j, i: (i, j + nb)),
                  pl.BlockSpec((tr, tc), lambda j, i: (i, j + 2 * nb)),
                  pl.BlockSpec((SC_WIDTH, tc), lambda j, i: (0, j))],
        out_specs=pl.BlockSpec((tr, tc), lambda j, i: (i, j)),
        out_shape=jax.ShapeDtypeStruct((m, d), BF16),
        scratch_shapes=[pltpu.VMEM((HALO, tc), F32)],
        compiler_params=_params(("arbitrary", "arbitrary"), blocks),
        name="sconv_gate",
    )(bcu, bcu, bcu, conv_w.astype(F32))


def _gdn_conv_kernel(x_ref, w_ref, o_ref, carry_ref, *, tiles_per_seq, n_l2_blocks, n_q_blocks):
    j = pl.program_id(0)
    first_tile = pl.program_id(1) % tiles_per_seq == 0

    def load_rows(r0):
        return x_ref[pl.ds(r0, CONV_SUB), :].astype(F32)

    def emit_plain(r0, y):
        o_ref[pl.ds(r0, CONV_SUB), :] = (y * jax.nn.sigmoid(y)).astype(o_ref.dtype)

    def emit_l2(r0, y):
        y = y * jax.nn.sigmoid(y)
        scale = jnp.where(j < n_q_blocks, GDN_DK ** -0.5, 1.0).astype(F32)
        for hh in range(y.shape[1] // GDN_DK):
            ys = y[:, hh * GDN_DK:(hh + 1) * GDN_DK]
            inv = lax.rsqrt(jnp.sum(ys * ys, axis=-1, keepdims=True) + L2_EPS) * scale
            o_ref[pl.ds(r0, CONV_SUB), hh * GDN_DK:(hh + 1) * GDN_DK] = (ys * inv).astype(o_ref.dtype)

    @pl.when(j < n_l2_blocks)
    def _():
        _conv_rows(load_rows, w_ref, carry_ref, o_ref.shape[0], first_tile, emit_l2)

    @pl.when(j >= n_l2_blocks)
    def _():
        _conv_rows(load_rows, w_ref, carry_ref, o_ref.shape[0], first_tile, emit_plain)


def _gdn_conv(proj, conv_w, t, out_dtype):
    m = proj.shape[0]
    tr = _row_tile(t, CONV_ROW_TILE)
    tc = CONV_COL_TILE
    blocks = _nbytes((tr, tc), proj.dtype) + _nbytes((tr, tc), out_dtype) + _nbytes((GDN_CONV_WIDTH, tc), F32)
    return pl.pallas_call(
        functools.partial(_gdn_conv_kernel, tiles_per_seq=t // tr,
                          n_l2_blocks=2 * GDN_QK_DIM // tc, n_q_blocks=GDN_QK_DIM // tc),
        grid=(GDN_CONV_DIM // tc, m // tr),
        in_specs=[pl.BlockSpec((tr, tc), lambda j, i: (i, j)),
                  pl.BlockSpec((GDN_CONV_WIDTH, tc), lambda j, i: (0, j))],
        out_specs=pl.BlockSpec((tr, tc), lambda j, i: (i, j)),
        out_shape=jax.ShapeDtypeStruct((m, GDN_CONV_DIM), out_dtype),
        scratch_shapes=[pltpu.VMEM((HALO, tc), F32)],
        compiler_params=_params(("arbitrary", "arbitrary"), blocks),
        name="gdn_conv",
    )(proj, conv_w.astype(F32))


def _gdn_gates_kernel(ba_ref, alog_ref, dtb_ref, o_ref):
    x = ba_ref[...]
    z = x + dtb_ref[...]
    softplus = jnp.maximum(z, 0.0) + jnp.log1p(jnp.exp(-jnp.abs(z)))
    g = -jnp.exp(alog_ref[...]) * softplus
    pos = lax.broadcasted_iota(jnp.int32, x.shape, 0) % CHUNK
    s = 1
    while s < CHUNK:
        g = g + jnp.where(pos >= s, pltpu.roll(g, s, 0), 0.0)
        s *= 2
    lane = lax.broadcasted_iota(jnp.int32, x.shape, 1)
    o_ref[...] = jnp.where(lane < GDN_NV, jax.nn.sigmoid(x), g)


def _gdn_gates(ba, a_log, dt_bias, t):
    m = ba.shape[0]
    tr = _row_tile(t, 1040, CHUNK)
    pad = LANES - 2 * GDN_NV
    alog = jnp.concatenate([jnp.zeros((GDN_NV,), F32), a_log.astype(F32), jnp.zeros((pad,), F32)])
    dtb = jnp.concatenate([jnp.zeros((GDN_NV,), F32), dt_bias.astype(F32), jnp.zeros((pad,), F32)])
    return pl.pallas_call(
        _gdn_gates_kernel,
        grid=(m // tr,),
        in_specs=[pl.BlockSpec((tr, LANES), lambda i: (i, 0)),
                  pl.BlockSpec((1, LANES), lambda i: (0, 0)),
                  pl.BlockSpec((1, LANES), lambda i: (0, 0))],
        out_specs=pl.BlockSpec((tr, LANES), lambda i: (i, 0)),
        out_shape=jax.ShapeDtypeStruct((m, LANES), F32),
        compiler_params=_params(("arbitrary",), 2 * _nbytes((tr, LANES), F32)),
        name="gdn_gates",
    )(ba, alog.reshape(1, LANES), dtb.reshape(1, LANES))


def _bdot(a, b):
    return jnp.dot(a.astype(BF16), b.astype(BF16), preferred_element_type=F32)


def _gdn_chunk_kernel(q_ref, k_ref, v_ref, z_ref, col_ref, row_ref, nw_ref, o_ref, s_ref):
    n_chunks = q_ref.shape[0] // CHUNK
    n_heads = HEADS_PER_STEP
    rep = n_heads // KHEADS_PER_STEP
    chunks = range(n_chunks)
    heads = range(n_heads)
    pairs = [(c, h) for c in chunks for h in heads]

    @pl.when(pl.program_id(2) == 0)
    def _():
        s_ref[...] = jnp.zeros_like(s_ref)

    ci = lax.broadcasted_iota(jnp.int32, (CHUNK, CHUNK), 0)
    si = lax.broadcasted_iota(jnp.int32, (CHUNK, CHUNK), 1)
    causal = ci >= si
    strict = ci > si
    blk16 = (ci >> 4) == (si >> 4)
    blk32 = (ci >> 5) == (si >> 5)
    eye = (ci == si).astype(F32)
    upper_half = lax.broadcasted_iota(jnp.int32, (CHUNK, 2 * CHUNK), 1) >= CHUNK
    nw = nw_ref[...]

    def rows(c):
        return slice(c * CHUNK, (c + 1) * CHUNK)

    def hcols(h, width):
        return slice(h * width, (h + 1) * width)

    qc = {(c, kh): q_ref[rows(c), hcols(kh, GDN_DK)].astype(F32)
          for c in chunks for kh in range(KHEADS_PER_STEP)}
    kc = {(c, kh): k_ref[rows(c), hcols(kh, GDN_DK)].astype(F32)
          for c in chunks for kh in range(KHEADS_PER_STEP)}
    beta = {(c, h): col_ref[0, rows(c), h:h + 1] for c, h in pairs}
    gcol = {(c, h): col_ref[0, rows(c), n_heads + h:n_heads + h + 1] for c, h in pairs}
    grow = {(c, h): row_ref[0, c, h:h + 1, :] for c, h in pairs}
    glast = {p: gcol[p][CHUNK - 1:CHUNK, :] for p in pairs}
    eg = {p: jnp.exp(gcol[p]) for p in pairs}
    decay = {p: jnp.where(causal, jnp.exp(gcol[p] - grow[p]), 0.0) for p in pairs}
    kb = {(c, h): kc[c, h // rep] * beta[c, h] for c, h in pairs}

    kk = {}
    for c in chunks:
        for kh in range(KHEADS_PER_STEP):
            lhs = jnp.concatenate([kb[c, kh * rep + r] for r in range(rep)] + [qc[c, kh]], axis=0)
            kk[c, kh] = lax.dot_general(lhs.astype(BF16), kc[c, kh].astype(BF16),
                                        (((1,), (1,)), ((), ())), preferred_element_type=F32)

    nmat = {(c, h): jnp.where(strict, -kk[c, h // rep][(h % rep) * CHUNK:(h % rep + 1) * CHUNK]
                              * decay[c, h], 0.0) for c, h in pairs}
    qk = {(c, h): jnp.where(causal, kk[c, h // rep][rep * CHUNK:] * decay[c, h], 0.0).astype(BF16)
          for c, h in pairs}
    nd = {p: jnp.where(blk16, nmat[p], 0.0) for p in pairs}
    no1 = {p: (jnp.where(blk32, nmat[p], 0.0) - nd[p]).astype(BF16) for p in pairs}
    no2 = {p: jnp.where(blk32, 0.0, nmat[p]).astype(BF16) for p in pairs}
    p1 = {p: _bdot(nd[p], nd[p]) for p in pairs}
    ps = {p: jnp.concatenate([p1[p], eye + nd[p]], axis=1) for p in pairs}
    for _ in range(2):
        ps = {p: _bdot(ps[p][:, :CHUNK], ps[p]) + jnp.where(upper_half, ps[p], 0.0) for p in pairs}
    t16 = {p: (_bdot(ps[p][:, :CHUNK], ps[p]) + ps[p])[:, CHUNK:] for p in pairs}
    x1 = {p: jnp.dot(no1[p], t16[p].astype(BF16), preferred_element_type=F32) for p in pairs}
    tm = {p: (t16[p] + _bdot(t16[p], x1[p]) - eye).astype(BF16) for p in pairs}
    rhs = {(c, h): jnp.concatenate([v_ref[rows(c), hcols(h, GDN_DV)].astype(F32) * beta[c, h],
                                    kb[c, h] * eg[c, h]], axis=1) for c, h in pairs}
    y = {p: rhs[p] + jnp.dot(tm[p], rhs[p].astype(BF16), preferred_element_type=F32) for p in pairs}
    zz = {p: jnp.dot(no2[p], y[p].astype(BF16), preferred_element_type=F32) for p in pairs}
    uw = {p: y[p] + zz[p] + jnp.dot(tm[p], zz[p].astype(BF16), preferred_element_type=F32)
          for p in pairs}
    wq = {(c, h): jnp.concatenate([uw[c, h][:, GDN_DV:], qc[c, h // rep] * eg[c, h]],
                                  axis=0).astype(BF16) for c, h in pairs}
    ktail = {(c, h): (kc[c, h // rep] * jnp.exp(glast[c, h] - gcol[c, h])).astype(BF16)
             for c, h in pairs}

    state = [s_ref[h] for h in heads]
    for c in chunks:
        ws_qs = [jnp.dot(wq[c, h], state[h].astype(BF16), preferred_element_type=F32) for h in heads]
        v_new = [(uw[c, h][:, :GDN_DV] - ws_qs[h][:CHUNK]).astype(BF16) for h in heads]
        o2 = [jnp.dot(qk[c, h], v_new[h], preferred_element_type=F32) for h in heads]
        ds = [lax.dot_general(ktail[c, h], v_new[h], (((0,), (0,)), ((), ())),
                              preferred_element_type=F32) for h in heads]
        state = [state[h] * jnp.exp(glast[c, h]) + ds[h] for h in heads]
        for h in heads:
            o = ws_qs[h][CHUNK:] + o2[h]
            zc = z_ref[rows(c), hcols(h, GDN_DV)].astype(F32)
            on = o * lax.rsqrt(jnp.mean(o * o, axis=-1, keepdims=True) + NORM_EPS) * nw
            o_ref[rows(c), hcols(h, GDN_DV)] = (on * (zc * jax.nn.sigmoid(zc))).astype(o_ref.dtype)
    for h in heads:
        s_ref[h] = state[h]


def _gdn_chunk(qkv, proj, gates, norm_w, bsz, t):
    m = qkv.shape[0]
    hs, khs = HEADS_PER_STEP, KHEADS_PER_STEP
    n_groups = GDN_NV // hs
    rb = _row_tile(t, GDN_CHUNKS_PER_STEP * CHUNK, CHUNK)
    nrb = t // rb
    ncb = rb // CHUNK
    n_chunks_total = m // CHUNK
    beta = gates[:, :GDN_NV].reshape(m, n_groups, hs)
    gc = gates[:, GDN_NV:2 * GDN_NV].reshape(m, n_groups, hs)
    col = jnp.transpose(jnp.concatenate([beta, gc], axis=-1), (1, 0, 2))
    row = jnp.transpose(gc.reshape(n_chunks_total, CHUNK, n_groups, hs), (2, 0, 3, 1))

    kq = khs * GDN_DK
    vq = hs * GDN_DV
    q_off = 0
    k_off = GDN_QK_DIM // kq
    v_off = 2 * GDN_QK_DIM // vq
    z_off = GDN_CONV_DIM // vq
    blocks = (2 * _nbytes((rb, kq), qkv.dtype) + _nbytes((rb, vq), qkv.dtype)
              + _nbytes((rb, vq), proj.dtype) + _nbytes((rb, LANES), F32)
              + _nbytes((ncb, SUBLANES, LANES), F32) + _nbytes((rb, vq), BF16))
    return pl.pallas_call(
        _gdn_chunk_kernel,
        grid=(bsz, n_groups, nrb),
        in_specs=[pl.BlockSpec((rb, kq), lambda b, g, r: (b * nrb + r, q_off + g)),
                  pl.BlockSpec((rb, kq), lambda b, g, r: (b * nrb + r, k_off + g)),
                  pl.BlockSpec((rb, vq), lambda b, g, r: (b * nrb + r, v_off + g)),
                  pl.BlockSpec((rb, vq), lambda b, g, r: (b * nrb + r, z_off + g)),
                  pl.BlockSpec((1, rb, 2 * hs), lambda b, g, r: (g, b * nrb + r, 0)),
                  pl.BlockSpec((1, ncb, hs, CHUNK), lambda b, g, r: (g, b * nrb + r, 0, 0)),
                  pl.BlockSpec((1, GDN_DV), lambda b, g, r: (0, 0))],
        out_specs=pl.BlockSpec((rb, vq), lambda b, g, r: (b * nrb + r, g)),
        out_shape=jax.ShapeDtypeStruct((m, GDN_V_DIM), BF16),
        scratch_shapes=[pltpu.VMEM((hs, GDN_DK, GDN_DV), F32)],
        compiler_params=_params(("arbitrary", "arbitrary", "arbitrary"), blocks,
                                _nbytes((hs, GDN_DK, GDN_DV), F32)),
        name="gdn_chunk",
    )(qkv, qkv, qkv, proj, col, row, norm_w.reshape(1, GDN_DV).astype(F32))


def _short_conv_mixer(h, hn, w_in, conv_w, w_out, j, t):
    bcu = _mm_plain(hn, w_in, j, BF16, t)
    y = _sconv_gate(bcu, conv_w[j], t)
    return _mm_residual(y, w_out, j, h, t)


def _gated_deltanet_mixer(h, hn, w_in, conv_w, a_log, dt_bias, norm_w, w_out, j, bsz, t):
    main_cols = GDN_CONV_DIM + GDN_V_DIM
    w_in_t = jnp.swapaxes(w_in, 1, 2)
    proj = _mm_plain_t(hn, w_in_t, j, BF16, t, main_cols)
    ba = _mm_tail_t(hn, w_in_t, j, main_cols, t)
    gates = _gdn_gates(ba, a_log[j], dt_bias[j], t)
    qkv = _gdn_conv(proj, conv_w[j], t, BF16)
    og = _gdn_chunk(qkv, proj, gates, norm_w[j], bsz, t)
    return _mm_residual(og, w_out, j, h, t)


def _ffn(h, hn, w_gate, w_up, w_down, i, t):
    a = _mm_swiglu(hn, w_gate, w_up, i, t)
    return _mm_residual(a, w_down, i, h, t)


def kernel(x, meta_tokens, mixer_norm, ffn_norm, sc_w_in, sc_conv_w, sc_w_out, gdn_w_in, gdn_conv_w, gdn_a_log, gdn_dt_bias, gdn_norm_w, gdn_w_out, ffn_w_gate, ffn_w_up, ffn_w_down, final_norm):
    bsz, seq, d = x.shape
    depth = mixer_norm.shape[0]
    t = PAD_FRONT + N_META + seq
    assert t % CHUNK == 0
    h, hn = _embed(x, meta_tokens, mixer_norm[0], t)
    for i in range(depth):
        j = i // 2
        if i > 0:
            hn = _rmsnorm(h, mixer_norm[i], BF16, t)
        if i % 2 == 0:
            h = _short_conv_mixer(h, hn, sc_w_in, sc_conv_w, sc_w_out, j, t)
        else:
            h = _gated_deltanet_mixer(h, hn, gdn_w_in, gdn_conv_w, gdn_a_log, gdn_dt_bias,
                                      gdn_norm_w, gdn_w_out, j, bsz, t)
        hn = _rmsnorm(h, ffn_norm[i], BF16, t)
        h = _ffn(h, hn, ffn_w_gate, ffn_w_up, ffn_w_down, i, t)
    return _final_rmsnorm(h, final_norm, bsz, t, seq)
```

```python
import functools
import math

import jax
import jax.numpy as jnp
from jax import lax
from jax.experimental import pallas as pl
from jax.experimental.pallas import tpu as pltpu

F32 = jnp.float32
BF16 = jnp.bfloat16

N_META = 16
SC_WIDTH = 3
GDN_NK = 16
GDN_NV = 32
GDN_DK = 128
GDN_DV = 128
GDN_QK_DIM = GDN_NK * GDN_DK
GDN_V_DIM = GDN_NV * GDN_DV
GDN_CONV_DIM = 2 * GDN_QK_DIM + GDN_V_DIM
GDN_CONV_WIDTH = 4
CHUNK = 64
NORM_EPS = 1e-6
L2_EPS = 1e-6
PAD_FRONT = (-N_META) % CHUNK

LANES = 128
SUBLANES = 8
BF16_ROWS = 16
VMEM_BYTES_V7X = 64 * 1024 * 1024
VMEM_HEADROOM = 8 * 1024 * 1024

HEADS_PER_STEP = 8
KHEADS_PER_STEP = HEADS_PER_STEP * GDN_NK // GDN_NV
HALO = SUBLANES
GDN_CHUNKS_PER_STEP = 5
CONV_SUB = BF16_ROWS
CONV_UNROLL = 13
CONV_ROW_TILE = 4160
CONV_COL_TILE = 2 * LANES


def _row_tile(t, target, mult=BF16_ROWS):
    best = None
    for d in range(mult, min(t, target) + 1, mult):
        if t % d == 0:
            best = d
    assert best is not None, (t, target, mult)
    return best


def _col_tile(n, target):
    best = None
    for d in range(LANES, min(n, target) + 1, LANES):
        if n % d == 0:
            best = d
    assert best is not None, (n, target)
    return best


def _nbytes(shape, dtype):
    return math.prod(shape) * jnp.dtype(dtype).itemsize


def _params(semantics, block_bytes, scratch_bytes=0):
    need = 2 * block_bytes + scratch_bytes + VMEM_HEADROOM
    return pltpu.CompilerParams(
        dimension_semantics=semantics,
        vmem_limit_bytes=min(max(need, 32 * 1024 * 1024), VMEM_BYTES_V7X - 4 * 1024 * 1024),
    )


def _rmsnorm_kernel(h_ref, w_ref, o_ref):
    x = h_ref[...]
    y = x * lax.rsqrt(jnp.mean(x * x, axis=-1, keepdims=True) + NORM_EPS)
    o_ref[...] = (y * w_ref[...]).astype(o_ref.dtype)


def _rmsnorm(h, w, out_dtype, t):
    m, d = h.shape
    tr = _row_tile(t, 512)
    blocks = _nbytes((tr, d), F32) + _nbytes((tr, d), out_dtype)
    return pl.pallas_call(
        _rmsnorm_kernel,
        grid=(m // tr,),
        in_specs=[pl.BlockSpec((tr, d), lambda i: (i, 0)),
                  pl.BlockSpec((1, d), lambda i: (0, 0))],
        out_specs=pl.BlockSpec((tr, d), lambda i: (i, 0)),
        out_shape=jax.ShapeDtypeStruct((m, d), out_dtype),
        compiler_params=_params(("arbitrary",), blocks),
        name="rmsnorm",
    )(h, w.reshape(1, d).astype(F32))


def _embed_kernel(x_ref, meta_ref, w_ref, h_ref, hn_ref, *, skip):
    def finish(rows):
        h_ref[...] = rows
        y = rows * lax.rsqrt(jnp.mean(rows * rows, axis=-1, keepdims=True) + NORM_EPS)
        hn_ref[...] = (y * w_ref[...]).astype(hn_ref.dtype)

    tr, d = h_ref.shape

    @pl.when(pl.program_id(1) == 0)
    def _():
        finish(jnp.concatenate([jnp.zeros((skip - N_META, d), F32), meta_ref[...],
                                x_ref[:tr - skip, :]], axis=0))

    if x_ref.shape[0] == tr:
        @pl.when(pl.program_id(1) != 0)
        def _():
            finish(x_ref[...])


def _embed(x, meta_tokens, w, t):
    bsz, seq, d = x.shape
    skip = t - seq
    tr = _row_tile(t, 1040)
    tiles = t // tr
    xr = tr if tiles > 1 else seq
    assert skip < tr and xr <= seq
    blocks = _nbytes((xr, d), F32) + _nbytes((tr, d), F32) + _nbytes((tr, d), BF16)
    return pl.pallas_call(
        functools.partial(_embed_kernel, skip=skip),
        grid=(bsz, tiles),
        in_specs=[pl.BlockSpec((pl.Element(xr), pl.Element(d)),
                               lambda b, k: (pl.multiple_of(
                                   b * seq + jnp.maximum(k * tr - skip, 0), SUBLANES), 0)),
                  pl.BlockSpec((N_META, d), lambda b, k: (0, 0)),
                  pl.BlockSpec((1, d), lambda b, k: (0, 0))],
        out_specs=[pl.BlockSpec((tr, d), lambda b, k: (b * tiles + k, 0)),
                   pl.BlockSpec((tr, d), lambda b, k: (b * tiles + k, 0))],
        out_shape=[jax.ShapeDtypeStruct((bsz * t, d), F32),
                   jax.ShapeDtypeStruct((bsz * t, d), BF16)],
        compiler_params=_params(("arbitrary", "arbitrary"), blocks),
        name="embed",
    )(x.reshape(bsz * seq, d), meta_tokens.astype(F32), w.reshape(1, d).astype(F32))


def _final_rmsnorm(h, w, bsz, t, seq):
    d = h.shape[1]
    tr = _row_tile(seq, 512)
    nt = seq // tr
    skip = t - seq
    blocks = 2 * _nbytes((tr, d), F32)
    out = pl.pallas_call(
        _rmsnorm_kernel,
        grid=(bsz, nt),
        in_specs=[pl.BlockSpec((pl.Element(tr), pl.Element(d)),
                               lambda b, i: (pl.multiple_of(b * t + skip + i * tr, SUBLANES), 0)),
                  pl.BlockSpec((1, d), lambda b, i: (0, 0))],
        out_specs=pl.BlockSpec((tr, d), lambda b, i: (b * nt + i, 0)),
        out_shape=jax.ShapeDtypeStruct((bsz * seq, d), F32),
        compiler_params=_params(("arbitrary", "arbitrary"), blocks),
        name="final_rmsnorm",
    )(h, w.reshape(1, d).astype(F32))
    return out.reshape(bsz, seq, d)


def _cast_weight(w_ref, wb_ref):
    @pl.when(pl.program_id(1) == 0)
    def _():
        wb_ref[...] = w_ref[...].astype(wb_ref.dtype)


def _mm_plain_kernel(x_ref, w_ref, o_ref, wb_ref):
    _cast_weight(w_ref, wb_ref)
    acc = jnp.dot(x_ref[...], wb_ref[...], preferred_element_type=F32)
    o_ref[...] = acc.astype(o_ref.dtype)


def _mm_residual_kernel(x_ref, w_ref, r_ref, o_ref, wb_ref):
    _cast_weight(w_ref, wb_ref)
    acc = jnp.dot(x_ref[...], wb_ref[...], preferred_element_type=F32)
    o_ref[...] = r_ref[...] + acc


def _mm_swiglu_kernel(x_ref, wg_ref, wu_ref, o_ref, wgb_ref, wub_ref):
    _cast_weight(wg_ref, wgb_ref)
    _cast_weight(wu_ref, wub_ref)
    x = x_ref[...]
    g = jnp.dot(x, wgb_ref[...], preferred_element_type=F32)
    u = jnp.dot(x, wub_ref[...], preferred_element_type=F32)
    o_ref[...] = (g * jax.nn.sigmoid(g) * u).astype(o_ref.dtype)


_NT_DIMS = (((1,), (1,)), ((), ()))


def _mm_plain_t_kernel(x_ref, wt_ref, o_ref, wb_ref):
    _cast_weight(wt_ref, wb_ref)
    acc = lax.dot_general(x_ref[...], wb_ref[...], _NT_DIMS, preferred_element_type=F32)
    o_ref[...] = acc.astype(o_ref.dtype)


def _mm_tail_t_kernel(x_ref, wt_ref, o_ref, wb_ref):
    @pl.when(pl.program_id(0) == 0)
    def _():
        wb_ref[...] = jnp.zeros_like(wb_ref)
        wb_ref[:wt_ref.shape[0], :] = wt_ref[...].astype(wb_ref.dtype)

    o_ref[...] = lax.dot_general(x_ref[...], wb_ref[...], _NT_DIMS, preferred_element_type=F32)


def _mm_tiles(t, k, n, n_weights):
    tn = _col_tile(n, 1024 if k <= 2048 and n_weights == 1 else 512)
    tm = _row_tile(t, 1040 if k <= 4096 else 640)
    return tm, tn


def _mm_plain(x, w, layer, out_dtype, t, n_cols=None):
    m, k = x.shape
    n = w.shape[2] if n_cols is None else n_cols
    tm, tn = _mm_tiles(t, k, n, 1)
    blocks = (_nbytes((tm, k), BF16) + _nbytes((k, tn), F32) + _nbytes((tm, tn), out_dtype))
    return pl.pallas_call(
        _mm_plain_kernel,
        grid=(n // tn, m // tm),
        in_specs=[pl.BlockSpec((tm, k), lambda j, i: (i, 0)),
                  pl.BlockSpec((None, k, tn), lambda j, i: (layer, 0, j))],
        out_specs=pl.BlockSpec((tm, tn), lambda j, i: (i, j)),
        out_shape=jax.ShapeDtypeStruct((m, n), out_dtype),
        scratch_shapes=[pltpu.VMEM((k, tn), BF16)],
        compiler_params=_params(("arbitrary", "arbitrary"), blocks, _nbytes((k, tn), BF16)),
        name="mm_plain",
    )(x, w)


def _mm_plain_t(x, wt, layer, out_dtype, t, n_cols):
    m, k = x.shape
    tm, tn = _mm_tiles(t, k, n_cols, 1)
    blocks = (_nbytes((tm, k), BF16) + _nbytes((tn, k), F32) + _nbytes((tm, tn), out_dtype))
    return pl.pallas_call(
        _mm_plain_t_kernel,
        grid=(n_cols // tn, m // tm),
        in_specs=[pl.BlockSpec((tm, k), lambda j, i: (i, 0)),
                  pl.BlockSpec((None, tn, k), lambda j, i: (layer, j, 0))],
        out_specs=pl.BlockSpec((tm, tn), lambda j, i: (i, j)),
        out_shape=jax.ShapeDtypeStruct((m, n_cols), out_dtype),
        scratch_shapes=[pltpu.VMEM((tn, k), BF16)],
        compiler_params=_params(("arbitrary", "arbitrary"), blocks, _nbytes((tn, k), BF16)),
        name="mm_plain_t",
    )(x, wt)


def _mm_tail_t(x, wt, layer, row_start, t):
    m, k = x.shape
    valid = wt.shape[1] - row_start
    assert 0 < valid <= LANES and row_start % valid == 0 and valid % BF16_ROWS == 0
    tm = _row_tile(t, 1040)
    blocks = _nbytes((tm, k), BF16) + _nbytes((valid, k), F32) + _nbytes((tm, LANES), F32)
    return pl.pallas_call(
        _mm_tail_t_kernel,
        grid=(m // tm,),
        in_specs=[pl.BlockSpec((tm, k), lambda i: (i, 0)),
                  pl.BlockSpec((None, valid, k), lambda i: (layer, row_start // valid, 0))],
        out_specs=pl.BlockSpec((tm, LANES), lambda i: (i, 0)),
        out_shape=jax.ShapeDtypeStruct((m, LANES), F32),
        scratch_shapes=[pltpu.VMEM((LANES, k), BF16)],
        compiler_params=_params(("arbitrary",), blocks, _nbytes((LANES, k), BF16)),
        name="mm_tail_t",
    )(x, wt)


def _mm_residual(x, w, layer, res, t):
    m, k = x.shape
    n = w.shape[2]
    tm, tn = _mm_tiles(t, k, n, 1)
    blocks = (_nbytes((tm, k), BF16) + _nbytes((k, tn), F32) + 2 * _nbytes((tm, tn), F32))
    return pl.pallas_call(
        _mm_residual_kernel,
        grid=(n // tn, m // tm),
        in_specs=[pl.BlockSpec((tm, k), lambda j, i: (i, 0)),
                  pl.BlockSpec((None, k, tn), lambda j, i: (layer, 0, j)),
                  pl.BlockSpec((tm, tn), lambda j, i: (i, j))],
        out_specs=pl.BlockSpec((tm, tn), lambda j, i: (i, j)),
        out_shape=jax.ShapeDtypeStruct((m, n), F32),
        scratch_shapes=[pltpu.VMEM((k, tn), BF16)],
        compiler_params=_params(("arbitrary", "arbitrary"), blocks, _nbytes((k, tn), BF16)),
        name="mm_residual",
    )(x, w, res)


def _mm_swiglu(x, wg, wu, layer, t):
    m, k = x.shape
    n = wg.shape[2]
    tm, tn = _mm_tiles(t, k, n, 2)
    blocks = (_nbytes((tm, k), BF16) + 2 * _nbytes((k, tn), F32) + _nbytes((tm, tn), BF16))
    return pl.pallas_call(
        _mm_swiglu_kernel,
        grid=(n // tn, m // tm),
        in_specs=[pl.BlockSpec((tm, k), lambda j, i: (i, 0)),
                  pl.BlockSpec((None, k, tn), lambda j, i: (layer, 0, j)),
                  pl.BlockSpec((None, k, tn), lambda j, i: (layer, 0, j))],
        out_specs=pl.BlockSpec((tm, tn), lambda j, i: (i, j)),
        out_shape=jax.ShapeDtypeStruct((m, n), BF16),
        scratch_shapes=[pltpu.VMEM((k, tn), BF16), pltpu.VMEM((k, tn), BF16)],
        compiler_params=_params(("arbitrary", "arbitrary"), blocks, 2 * _nbytes((k, tn), BF16)),
        name="mm_swiglu",
    )(x, wg, wu)


def _conv_rows(load_rows, w_ref, carry_ref, n_rows, first_tile, emit):
    @pl.when(first_tile)
    def _():
        carry_ref[...] = jnp.zeros_like(carry_ref)

    width = w_ref.shape[0]
    n_sub = n_rows // CONV_SUB
    unroll = max(u for u in range(1, CONV_UNROLL + 1) if n_sub % u == 0)
    taps = [jnp.broadcast_to(w_ref[j:j + 1, :], (CONV_SUB, w_ref.shape[1])) for j in range(width)]

    def body(i, prev):
        for k in range(unroll):
            r0 = pl.multiple_of((i * unroll + k) * CONV_SUB, CONV_SUB)
            cur = load_rows(r0)
            xe = jnp.concatenate([prev, cur], axis=0)
            y = taps[width - 1] * cur
            for s in range(1, width):
                y = y + taps[width - 1 - s] * xe[HALO - s:HALO - s + CONV_SUB]
            emit(r0, y)
            prev = cur[CONV_SUB - HALO:]
        return prev

    carry_ref[...] = lax.fori_loop(0, n_sub // unroll, body, carry_ref[...])


def _sconv_kernel(b_ref, c_ref, u_ref, w_ref, o_ref, carry_ref, *, tiles_per_seq):
    def load_rows(r0):
        rows = pl.ds(r0, CONV_SUB)
        return c_ref[rows, :].astype(F32) * u_ref[rows, :].astype(F32)

    def emit(r0, y):
        rows = pl.ds(r0, CONV_SUB)
        o_ref[rows, :] = (b_ref[rows, :].astype(F32) * y).astype(o_ref.dtype)

    _conv_rows(load_rows, w_ref, carry_ref, o_ref.shape[0],
               pl.program_id(1) % tiles_per_seq == 0, emit)


def _sconv_gate(bcu, conv_w, t):
    m = bcu.shape[0]
    d = bcu.shape[1] // 3
    tr = _row_tile(t, CONV_ROW_TILE)
    tc = CONV_COL_TILE
    nb = d // tc
    blocks = 4 * _nbytes((tr, tc), BF16) + _nbytes((SC_WIDTH, tc), F32)
    return pl.pallas_call(
        functools.partial(_sconv_kernel, tiles_per_seq=t // tr),
        grid=(nb, m // tr),
        in_specs=[pl.BlockSpec((tr, tc), lambda j, i: (i, j)),
                  pl.BlockSpec((tr, tc), lambda j, i: (i, j + nb)),
                  pl.BlockSpec((tr, tc), lambda j, i: (i, j + 2 * nb)),
                  pl.BlockSpec((SC_WIDTH, tc), lambda j, i: (0, j))],
        out_specs=pl.BlockSpec((tr, tc), lambda j, i: (i, j)),
        out_shape=jax.ShapeDtypeStruct((m, d), BF16),
        scratch_shapes=[pltpu.VMEM((HALO, tc), F32)],
        compiler_params=_params(("arbitrary", "arbitrary"), blocks),
        name="sconv_gate",
    )(bcu, bcu, bcu, conv_w.astype(F32))


def _gdn_conv_kernel(x_ref, w_ref, o_ref, carry_ref, *, tiles_per_seq, n_l2_blocks, n_q_blocks):
    j = pl.program_id(0)
    first_tile = pl.program_id(1) % tiles_per_seq == 0

    def load_rows(r0):
        return x_ref[pl.ds(r0, CONV_SUB), :].astype(F32)

    def emit_plain(r0, y):
        o_ref[pl.ds(r0, CONV_SUB), :] = (y * jax.nn.sigmoid(y)).astype(o_ref.dtype)

    def emit_l2(r0, y):
        y = y * jax.nn.sigmoid(y)
        scale = jnp.where(j < n_q_blocks, GDN_DK ** -0.5, 1.0).astype(F32)
        for hh in range(y.shape[1] // GDN_DK):
            ys = y[:, hh * GDN_DK:(hh + 1) * GDN_DK]
            inv = lax.rsqrt(jnp.sum(ys * ys, axis=-1, keepdims=True) + L2_EPS) * scale
            o_ref[pl.ds(r0, CONV_SUB), hh * GDN_DK:(hh + 1) * GDN_DK] = (ys * inv).astype(o_ref.dtype)

    @pl.when(j < n_l2_blocks)
    def _():
        _conv_rows(load_rows, w_ref, carry_ref, o_ref.shape[0], first_tile, emit_l2)

    @pl.when(j >= n_l2_blocks)
    def _():
        _conv_rows(load_rows, w_ref, carry_ref, o_ref.shape[0], first_tile, emit_plain)


def _gdn_conv(proj, conv_w, t, out_dtype):
    m = proj.shape[0]
    tr = _row_tile(t, CONV_ROW_TILE)
    tc = CONV_COL_TILE
    blocks = _nbytes((tr, tc), proj.dtype) + _nbytes((tr, tc), out_dtype) + _nbytes((GDN_CONV_WIDTH, tc), F32)
    return pl.pallas_call(
        functools.partial(_gdn_conv_kernel, tiles_per_seq=t // tr,
                          n_l2_blocks=2 * GDN_QK_DIM // tc, n_q_blocks=GDN_QK_DIM // tc),
        grid=(GDN_CONV_DIM // tc, m // tr),
        in_specs=[pl.BlockSpec((tr, tc), lambda j, i: (i, j)),
                  pl.BlockSpec((GDN_CONV_WIDTH, tc), lambda j, i: (0, j))],
        out_specs=pl.BlockSpec((tr, tc), lambda j, i: (i, j)),
        out_shape=jax.ShapeDtypeStruct((m, GDN_CONV_DIM), out_dtype),
        scratch_shapes=[pltpu.VMEM((HALO, tc), F32)],
        compiler_params=_params(("arbitrary", "arbitrary"), blocks),
        name="gdn_conv",
    )(proj, conv_w.astype(F32))


def _gdn_gates_kernel(ba_ref, alog_ref, dtb_ref, o_ref):
    x = ba_ref[...]
    z = x + dtb_ref[...]
    softplus = jnp.maximum(z, 0.0) + jnp.log1p(jnp.exp(-jnp.abs(z)))
    g = -jnp.exp(alog_ref[...]) * softplus
    pos = lax.broadcasted_iota(jnp.int32, x.shape, 0) % CHUNK
    s = 1
    while s < CHUNK:
        g = g + jnp.where(pos >= s, pltpu.roll(g, s, 0), 0.0)
        s *= 2
    lane = lax.broadcasted_iota(jnp.int32, x.shape, 1)
    o_ref[...] = jnp.where(lane < GDN_NV, jax.nn.sigmoid(x), g)


def _gdn_gates(ba, a_log, dt_bias, t):
    m = ba.shape[0]
    tr = _row_tile(t, 1040, CHUNK)
    pad = LANES - 2 * GDN_NV
    alog = jnp.concatenate([jnp.zeros((GDN_NV,), F32), a_log.astype(F32), jnp.zeros((pad,), F32)])
    dtb = jnp.concatenate([jnp.zeros((GDN_NV,), F32), dt_bias.astype(F32), jnp.zeros((pad,), F32)])
    return pl.pallas_call(
        _gdn_gates_kernel,
        grid=(m // tr,),
        in_specs=[pl.BlockSpec((tr, LANES), lambda i: (i, 0)),
                  pl.BlockSpec((1, LANES), lambda i: (0, 0)),
                  pl.BlockSpec((1, LANES), lambda i: (0, 0))],
        out_specs=pl.BlockSpec((tr, LANES), lambda i: (i, 0)),
        out_shape=jax.ShapeDtypeStruct((m, LANES), F32),
        compiler_params=_params(("arbitrary",), 2 * _nbytes((tr, LANES), F32)),
        name="gdn_gates",
    )(ba, alog.reshape(1, LANES), dtb.reshape(1, LANES))


def _bdot(a, b):
    return jnp.dot(a.astype(BF16), b.astype(BF16), preferred_element_type=F32)


def _gdn_chunk_kernel(q_ref, k_ref, v_ref, z_ref, col_ref, row_ref, nw_ref, o_ref, s_ref):
    n_chunks = q_ref.shape[0] // CHUNK
    n_heads = HEADS_PER_STEP
    rep = n_heads // KHEADS_PER_STEP
    chunks = range(n_chunks)
    heads = range(n_heads)
    pairs = [(c, h) for c in chunks for h in heads]

    @pl.when(pl.program_id(2) == 0)
    def _():
        s_ref[...] = jnp.zeros_like(s_ref)

    ci = lax.broadcasted_iota(jnp.int32, (CHUNK, CHUNK), 0)
    si = lax.broadcasted_iota(jnp.int32, (CHUNK, CHUNK), 1)
    causal = ci >= si
    strict = ci > si
    blk16 = (ci >> 4) == (si >> 4)
    blk32 = (ci >> 5) == (si >> 5)
    eye = (ci == si).astype(F32)
    upper_half = lax.broadcasted_iota(jnp.int32, (CHUNK, 2 * CHUNK), 1) >= CHUNK
    nw = nw_ref[...]

    def rows(c):
        return slice(c * CHUNK, (c + 1) * CHUNK)

    def hcols(h, width):
        return slice(h * width, (h + 1) * width)

    qc = {(c, kh): q_ref[rows(c), hcols(kh, GDN_DK)].astype(F32)
          for c in chunks for kh in range(KHEADS_PER_STEP)}
    kc = {(c, kh): k_ref[rows(c), hcols(kh, GDN_DK)].astype(F32)
          for c in chunks for kh in range(KHEADS_PER_STEP)}
    beta = {(c, h): col_ref[0, rows(c), h:h + 1] for c, h in pairs}
    gcol = {(c, h): col_ref[0, rows(c), n_heads + h:n_heads + h + 1] for c, h in pairs}
    grow = {(c, h): row_ref[0, c, h:h + 1, :] for c, h in pairs}
    glast = {p: gcol[p][CHUNK - 1:CHUNK, :] for p in pairs}
    eg = {p: jnp.exp(gcol[p]) for p in pairs}
    decay = {p: jnp.where(causal, jnp.exp(gcol[p] - grow[p]), 0.0) for p in pairs}
    kb = {(c, h): kc[c, h // rep] * beta[c, h] for c, h in pairs}

    kk = {}
    for c in chunks:
        for kh in range(KHEADS_PER_STEP):
            lhs = jnp.concatenate([kc[c, kh], qc[c, kh]], axis=0)
            kk[c, kh] = lax.dot_general(lhs.astype(BF16), kc[c, kh].astype(BF16),
                                        _NT_DIMS, preferred_element_type=F32)

    nmat = {(c, h): jnp.where(strict, -(kk[c, h // rep][:CHUNK] * beta[c, h]) * decay[c, h], 0.0)
            for c, h in pairs}
    qk = {(c, h): jnp.where(causal, kk[c, h // rep][CHUNK:] * decay[c, h], 0.0).astype(BF16)
          for c, h in pairs}
    nd = {p: jnp.where(blk16, nmat[p], 0.0) for p in pairs}
    no1 = {p: (jnp.where(blk32, nmat[p], 0.0) - nd[p]).astype(BF16) for p in pairs}
    no2 = {p: jnp.where(blk32, 0.0, nmat[p]).astype(BF16) for p in pairs}
    p1 = {p: _bdot(nd[p], nd[p]) for p in pairs}
    ps = {p: jnp.concatenate([p1[p], eye + nd[p]], axis=1) for p in pairs}
    for _ in range(2):
        ps = {p: _bdot(ps[p][:, :CHUNK], ps[p]) + jnp.where(upper_half, ps[p], 0.0) for p in pairs}
    t16 = {p: (_bdot(ps[p][:, :CHUNK], ps[p]) + ps[p])[:, CHUNK:] for p in pairs}
    x1 = {p: jnp.dot(no1[p], t16[p].astype(BF16), preferred_element_type=F32) for p in pairs}
    tm = {p: (t16[p] + _bdot(t16[p], x1[p]) - eye).astype(BF16) for p in pairs}
    rhs = {(c, h): jnp.concatenate([v_ref[rows(c), hcols(h, GDN_DV)].astype(F32) * beta[c, h],
                                    kb[c, h] * eg[c, h]], axis=1) for c, h in pairs}
    y = {p: rhs[p] + jnp.dot(tm[p], rhs[p].astype(BF16), preferred_element_type=F32) for p in pairs}
    zz = {p: jnp.dot(no2[p], y[p].astype(BF16), preferred_element_type=F32) for p in pairs}
    uw = {p: y[p] + zz[p] + jnp.dot(tm[p], zz[p].astype(BF16), preferred_element_type=F32)
          for p in pairs}
    wq = {(c, h): jnp.concatenate([uw[c, h][:, GDN_DV:], qc[c, h // rep] * eg[c, h]],
                                  axis=0).astype(BF16) for c, h in pairs}
    ktail = {(c, h): (kc[c, h // rep] * jnp.exp(glast[c, h] - gcol[c, h])).astype(BF16)
             for c, h in pairs}

    state = [s_ref[h] for h in heads]
    for c in chunks:
        ws_qs = [jnp.dot(wq[c, h], state[h].astype(BF16), preferred_element_type=F32) for h in heads]
        v_new = [(uw[c, h][:, :GDN_DV] - ws_qs[h][:CHUNK]).astype(BF16) for h in heads]
        o2 = [jnp.dot(qk[c, h], v_new[h], preferred_element_type=F32) for h in heads]
        ds = [lax.dot_general(ktail[c, h], v_new[h], (((0,), (0,)), ((), ())),
                              preferred_element_type=F32) for h in heads]
        state = [state[h] * jnp.exp(glast[c, h]) + ds[h] for h in heads]
        for h in heads:
            o = ws_qs[h][CHUNK:] + o2[h]
            zc = z_ref[rows(c), hcols(h, GDN_DV)].astype(F32)
            on = o * lax.rsqrt(jnp.mean(o * o, axis=-1, keepdims=True) + NORM_EPS) * nw
            o_ref[rows(c), hcols(h, GDN_DV)] = (on * (zc * jax.nn.sigmoid(zc))).astype(o_ref.dtype)
    for h in heads:
        s_ref[h] = state[h]


def _gdn_chunk(qkv, proj, gates, norm_w, bsz, t):
    m = qkv.shape[0]
    hs, khs = HEADS_PER_STEP, KHEADS_PER_STEP
    n_groups = GDN_NV // hs
    rb = _row_tile(t, GDN_CHUNKS_PER_STEP * CHUNK, CHUNK)
    nrb = t // rb
    ncb = rb // CHUNK
    n_chunks_total = m // CHUNK
    beta = gates[:, :GDN_NV].reshape(m, n_groups, hs)
    gc = gates[:, GDN_NV:2 * GDN_NV].reshape(m, n_groups, hs)
    col = jnp.transpose(jnp.concatenate([beta, gc], axis=-1), (1, 0, 2))
    row = jnp.transpose(gc.reshape(n_chunks_total, CHUNK, n_groups, hs), (2, 0, 3, 1))

    kq = khs * GDN_DK
    vq = hs * GDN_DV
    q_off = 0
    k_off = GDN_QK_DIM // kq
    v_off = 2 * GDN_QK_DIM // vq
    z_off = GDN_CONV_DIM // vq
    blocks = (2 * _nbytes((rb, kq), qkv.dtype) + _nbytes((rb, vq), qkv.dtype)
              + _nbytes((rb, vq), proj.dtype) + _nbytes((rb, LANES), F32)
              + _nbytes((ncb, SUBLANES, LANES), F32) + _nbytes((rb, vq), BF16))
    return pl.pallas_call(
        _gdn_chunk_kernel,
        grid=(bsz, n_groups, nrb),
        in_specs=[pl.BlockSpec((rb, kq), lambda b, g, r: (b * nrb + r, q_off + g)),
                  pl.BlockSpec((rb, kq), lambda b, g, r: (b * nrb + r, k_off + g)),
                  pl.BlockSpec((rb, vq), lambda b, g, r: (b * nrb + r, v_off + g)),
                  pl.BlockSpec((rb, vq), lambda b, g, r: (b * nrb + r, z_off + g)),
                  pl.BlockSpec((1, rb, 2 * hs), lambda b, g, r: (g, b * nrb + r, 0)),
                  pl.BlockSpec((1, ncb, hs, CHUNK), lambda b, g, r: (g, b * nrb + r, 0, 0)),
                  pl.BlockSpec((1, GDN_DV), lambda b, g, r: (0, 0))],
        out_specs=pl.BlockSpec((rb, vq), lambda b, g, r: (b * nrb + r, g)),
        out_shape=jax.ShapeDtypeStruct((m, GDN_V_DIM), BF16),
        scratch_shapes=[pltpu.VMEM((hs, GDN_DK, GDN_DV), F32)],
        compiler_params=_params(("arbitrary", "arbitrary", "arbitrary"), blocks,
                                _nbytes((hs, GDN_DK, GDN_DV), F32)),
        name="gdn_chunk",
    )(qkv, qkv, qkv, proj, col, row, norm_w.reshape(1, GDN_DV).astype(F32))


def _short_conv_mixer(h, hn, w_in, conv_w, w_out, j, t):
    bcu = _mm_plain(hn, w_in, j, BF16, t)
    y = _sconv_gate(bcu, conv_w[j], t)
    return _mm_residual(y, w_out, j, h, t)


def _gated_deltanet_mixer(h, hn, w_in, conv_w, a_log, dt_bias, norm_w, w_out, j, bsz, t):
    main_cols = GDN_CONV_DIM + GDN_V_DIM
    w_in_t = jnp.swapaxes(w_in, 1, 2)
    proj = _mm_plain_t(hn, w_in_t, j, BF16, t, main_cols)
    ba = _mm_tail_t(hn, w_in_t, j, main_cols, t)
    gates = _gdn_gates(ba, a_log[j], dt_bias[j], t)
    qkv = _gdn_conv(proj, conv_w[j], t, BF16)
    og = _gdn_chunk(qkv, proj, gates, norm_w[j], bsz, t)
    return _mm_residual(og, w_out, j, h, t)


def _ffn(h, hn, w_gate, w_up, w_down, i, t):
    a = _mm_swiglu(hn, w_gate, w_up, i, t)
    return _mm_residual(a, w_down, i, h, t)


def kernel(x, meta_tokens, mixer_norm, ffn_norm, sc_w_in, sc_conv_w, sc_w_out, gdn_w_in, gdn_conv_w, gdn_a_log, gdn_dt_bias, gdn_norm_w, gdn_w_out, ffn_w_gate, ffn_w_up, ffn_w_down, final_norm):
    bsz, seq, d = x.shape
    depth = mixer_norm.shape[0]
    t = PAD_FRONT + N_META + seq
    assert t % CHUNK == 0
    h, hn = _embed(x, meta_tokens, mixer_norm[0], t)
    for i in range(depth):
        j = i // 2
        if i > 0:
            hn = _rmsnorm(h, mixer_norm[i], BF16, t)
        if i % 2 == 0:
            h = _short_conv_mixer(h, hn, sc_w_in, sc_conv_w, sc_w_out, j, t)
        else:
            h = _gated_deltanet_mixer(h, hn, gdn_w_in, gdn_conv_w, gdn_a_log, gdn_dt_bias,
                                      gdn_norm_w, gdn_w_out, j, bsz, t)
        hn = _rmsnorm(h, ffn_norm[i], BF16, t)
        h = _ffn(h, hn, ffn_w_gate, ffn_w_up, ffn_w_down, i, t)
    return _final_rmsnorm(h, final_norm, bsz, t, seq)
```

```python
import functools
import math

import jax
import jax.numpy as jnp
from jax import lax
from jax.experimental import pallas as pl
from jax.experimental.pallas import tpu as pltpu

F32 = jnp.float32
BF16 = jnp.bfloat16

N_META = 16
SC_WIDTH = 3
GDN_NK = 16
GDN_NV = 32
GDN_DK = 128
GDN_DV = 128
GDN_QK_DIM = GDN_NK * GDN_DK
GDN_V_DIM = GDN_NV * GDN_DV
GDN_CONV_DIM = 2 * GDN_QK_DIM + GDN_V_DIM
GDN_CONV_WIDTH = 4
CHUNK = 64
NORM_EPS = 1e-6
L2_EPS = 1e-6
PAD_FRONT = (-N_META) % CHUNK

LANES = 128
SUBLANES = 8
BF16_ROWS = 16
VMEM_BYTES_V7X = 64 * 1024 * 1024
VMEM_HEADROOM = 8 * 1024 * 1024

HEADS_PER_STEP = 8
KHEADS_PER_STEP = HEADS_PER_STEP * GDN_NK // GDN_NV
HALO = SUBLANES
GDN_CHUNKS_PER_STEP = 13
GDN_STAGE_SKEW = 3
CONV_SUB = BF16_ROWS
CONV_UNROLL = 13
CONV_ROW_TILE = 4160
CONV_COL_TILE = 2 * LANES


def _row_tile(t, target, mult=BF16_ROWS):
    best = None
    for d in range(mult, min(t, target) + 1, mult):
        if t % d == 0:
            best = d
    assert best is not None, (t, target, mult)
    return best


def _col_tile(n, target):
    best = None
    for d in range(LANES, min(n, target) + 1, LANES):
        if n % d == 0:
            best = d
    assert best is not None, (n, target)
    return best


def _nbytes(shape, dtype):
    return math.prod(shape) * jnp.dtype(dtype).itemsize


def _params(semantics, block_bytes, scratch_bytes=0):
    need = 2 * block_bytes + scratch_bytes + VMEM_HEADROOM
    return pltpu.CompilerParams(
        dimension_semantics=semantics,
        vmem_limit_bytes=min(max(need, 32 * 1024 * 1024), VMEM_BYTES_V7X - 4 * 1024 * 1024),
    )


def _rmsnorm_kernel(h_ref, w_ref, o_ref):
    x = h_ref[...]
    y = x * lax.rsqrt(jnp.mean(x * x, axis=-1, keepdims=True) + NORM_EPS)
    o_ref[...] = (y * w_ref[...]).astype(o_ref.dtype)


def _rmsnorm(h, w, out_dtype, t):
    m, d = h.shape
    tr = _row_tile(t, 512)
    blocks = _nbytes((tr, d), F32) + _nbytes((tr, d), out_dtype)
    return pl.pallas_call(
        _rmsnorm_kernel,
        grid=(m // tr,),
        in_specs=[pl.BlockSpec((tr, d), lambda i: (i, 0)),
                  pl.BlockSpec((1, d), lambda i: (0, 0))],
        out_specs=pl.BlockSpec((tr, d), lambda i: (i, 0)),
        out_shape=jax.ShapeDtypeStruct((m, d), out_dtype),
        compiler_params=_params(("arbitrary",), blocks),
        name="rmsnorm",
    )(h, w.reshape(1, d).astype(F32))


def _embed_kernel(x_ref, meta_ref, w_ref, h_ref, hn_ref, *, skip):
    def finish(rows):
        h_ref[...] = rows
        y = rows * lax.rsqrt(jnp.mean(rows * rows, axis=-1, keepdims=True) + NORM_EPS)
        hn_ref[...] = (y * w_ref[...]).astype(hn_ref.dtype)

    tr, d = h_ref.shape

    @pl.when(pl.program_id(1) == 0)
    def _():
        finish(jnp.concatenate([jnp.zeros((skip - N_META, d), F32), meta_ref[...],
                                x_ref[:tr - skip, :]], axis=0))

    if x_ref.shape[0] == tr:
        @pl.when(pl.program_id(1) != 0)
        def _():
            finish(x_ref[...])


def _embed(x, meta_tokens, w, t):
    bsz, seq, d = x.shape
    skip = t - seq
    tr = _row_tile(t, 1040)
    tiles = t // tr
    xr = tr if tiles > 1 else seq
    assert skip < tr and xr <= seq
    blocks = _nbytes((xr, d), F32) + _nbytes((tr, d), F32) + _nbytes((tr, d), BF16)
    return pl.pallas_call(
        functools.partial(_embed_kernel, skip=skip),
        grid=(bsz, tiles),
        in_specs=[pl.BlockSpec((pl.Element(xr), pl.Element(d)),
                               lambda b, k: (pl.multiple_of(
                                   b * seq + jnp.maximum(k * tr - skip, 0), SUBLANES), 0)),
                  pl.BlockSpec((N_META, d), lambda b, k: (0, 0)),
                  pl.BlockSpec((1, d), lambda b, k: (0, 0))],
        out_specs=[pl.BlockSpec((tr, d), lambda b, k: (b * tiles + k, 0)),
                   pl.BlockSpec((tr, d), lambda b, k: (b * tiles + k, 0))],
        out_shape=[jax.ShapeDtypeStruct((bsz * t, d), F32),
                   jax.ShapeDtypeStruct((bsz * t, d), BF16)],
        compiler_params=_params(("arbitrary", "arbitrary"), blocks),
        name="embed",
    )(x.reshape(bsz * seq, d), meta_tokens.astype(F32), w.reshape(1, d).astype(F32))


def _final_rmsnorm(h, w, bsz, t, seq):
    d = h.shape[1]
    tr = _row_tile(seq, 512)
    nt = seq // tr
    skip = t - seq
    blocks = 2 * _nbytes((tr, d), F32)
    out = pl.pallas_call(
        _rmsnorm_kernel,
        grid=(bsz, nt),
        in_specs=[pl.BlockSpec((pl.Element(tr), pl.Element(d)),
                               lambda b, i: (pl.multiple_of(b * t + skip + i * tr, SUBLANES), 0)),
                  pl.BlockSpec((1, d), lambda b, i: (0, 0))],
        out_specs=pl.BlockSpec((tr, d), lambda b, i: (b * nt + i, 0)),
        out_shape=jax.ShapeDtypeStruct((bsz * seq, d), F32),
        compiler_params=_params(("arbitrary", "arbitrary"), blocks),
        name="final_rmsnorm",
    )(h, w.reshape(1, d).astype(F32))
    return out.reshape(bsz, seq, d)


def _cast_weight(w_ref, wb_ref):
    @pl.when(pl.program_id(1) == 0)
    def _():
        wb_ref[...] = w_ref[...].astype(wb_ref.dtype)


def _mm_plain_kernel(x_ref, w_ref, o_ref, wb_ref):
    _cast_weight(w_ref, wb_ref)
    acc = jnp.dot(x_ref[...], wb_ref[...], preferred_element_type=F32)
    o_ref[...] = acc.astype(o_ref.dtype)


def _mm_residual_kernel(x_ref, w_ref, r_ref, o_ref, wb_ref):
    _cast_weight(w_ref, wb_ref)
    acc = jnp.dot(x_ref[...], wb_ref[...], preferred_element_type=F32)
    o_ref[...] = r_ref[...] + acc


def _mm_swiglu_kernel(x_ref, wg_ref, wu_ref, o_ref, wgb_ref, wub_ref):
    _cast_weight(wg_ref, wgb_ref)
    _cast_weight(wu_ref, wub_ref)
    x = x_ref[...]
    g = jnp.dot(x, wgb_ref[...], preferred_element_type=F32)
    u = jnp.dot(x, wub_ref[...], preferred_element_type=F32)
    o_ref[...] = (g * jax.nn.sigmoid(g) * u).astype(o_ref.dtype)


_NT_DIMS = (((1,), (1,)), ((), ()))


def _mm_plain_t_kernel(x_ref, wt_ref, o_ref, wb_ref):
    _cast_weight(wt_ref, wb_ref)
    acc = lax.dot_general(x_ref[...], wb_ref[...], _NT_DIMS, preferred_element_type=F32)
    o_ref[...] = acc.astype(o_ref.dtype)


def _mm_tail_t_kernel(x_ref, wt_ref, o_ref, wb_ref):
    @pl.when(pl.program_id(0) == 0)
    def _():
        wb_ref[...] = jnp.zeros_like(wb_ref)
        wb_ref[:wt_ref.shape[0], :] = wt_ref[...].astype(wb_ref.dtype)

    o_ref[...] = lax.dot_general(x_ref[...], wb_ref[...], _NT_DIMS, preferred_element_type=F32)


def _mm_tiles(t, k, n, n_weights):
    tn = _col_tile(n, 1024 if k <= 2048 and n_weights == 1 else 512)
    tm = _row_tile(t, 1040 if k <= 4096 else 640)
    return tm, tn


def _mm_plain(x, w, layer, out_dtype, t, n_cols=None):
    m, k = x.shape
    n = w.shape[2] if n_cols is None else n_cols
    tm, tn = _mm_tiles(t, k, n, 1)
    blocks = (_nbytes((tm, k), BF16) + _nbytes((k, tn), F32) + _nbytes((tm, tn), out_dtype))
    return pl.pallas_call(
        _mm_plain_kernel,
        grid=(n // tn, m // tm),
        in_specs=[pl.BlockSpec((tm, k), lambda j, i: (i, 0)),
                  pl.BlockSpec((None, k, tn), lambda j, i: (layer, 0, j))],
        out_specs=pl.BlockSpec((tm, tn), lambda j, i: (i, j)),
        out_shape=jax.ShapeDtypeStruct((m, n), out_dtype),
        scratch_shapes=[pltpu.VMEM((k, tn), BF16)],
        compiler_params=_params(("arbitrary", "arbitrary"), blocks, _nbytes((k, tn), BF16)),
        name="mm_plain",
    )(x, w)


def _mm_plain_t(x, wt, layer, out_dtype, t, n_cols):
    m, k = x.shape
    tm, tn = _mm_tiles(t, k, n_cols, 1)
    blocks = (_nbytes((tm, k), BF16) + _nbytes((tn, k), F32) + _nbytes((tm, tn), out_dtype))
    return pl.pallas_call(
        _mm_plain_t_kernel,
        grid=(n_cols // tn, m // tm),
        in_specs=[pl.BlockSpec((tm, k), lambda j, i: (i, 0)),
                  pl.BlockSpec((None, tn, k), lambda j, i: (layer, j, 0))],
        out_specs=pl.BlockSpec((tm, tn), lambda j, i: (i, j)),
        out_shape=jax.ShapeDtypeStruct((m, n_cols), out_dtype),
        scratch_shapes=[pltpu.VMEM((tn, k), BF16)],
        compiler_params=_params(("arbitrary", "arbitrary"), blocks, _nbytes((tn, k), BF16)),
        name="mm_plain_t",
    )(x, wt)


def _mm_tail_t(x, wt, layer, row_start, t):
    m, k = x.shape
    valid = wt.shape[1] - row_start
    assert 0 < valid <= LANES and row_start % valid == 0 and valid % BF16_ROWS == 0
    tm = _row_tile(t, 1040)
    blocks = _nbytes((tm, k), BF16) + _nbytes((valid, k), F32) + _nbytes((tm, LANES), F32)
    return pl.pallas_call(
        _mm_tail_t_kernel,
        grid=(m // tm,),
        in_specs=[pl.BlockSpec((tm, k), lambda i: (i, 0)),
                  pl.BlockSpec((None, valid, k), lambda i: (layer, row_start // valid, 0))],
        out_specs=pl.BlockSpec((tm, LANES), lambda i: (i, 0)),
        out_shape=jax.ShapeDtypeStruct((m, LANES), F32),
        scratch_shapes=[pltpu.VMEM((LANES, k), BF16)],
        compiler_params=_params(("arbitrary",), blocks, _nbytes((LANES, k), BF16)),
        name="mm_tail_t",
    )(x, wt)


def _mm_residual(x, w, layer, res, t):
    m, k = x.shape
    n = w.shape[2]
    tm, tn = _mm_tiles(t, k, n, 1)
    blocks = (_nbytes((tm, k), BF16) + _nbytes((k, tn), F32) + 2 * _nbytes((tm, tn), F32))
    return pl.pallas_call(
        _mm_residual_kernel,
        grid=(n // tn, m // tm),
        in_specs=[pl.BlockSpec((tm, k), lambda j, i: (i, 0)),
                  pl.BlockSpec((None, k, tn), lambda j, i: (layer, 0, j)),
                  pl.BlockSpec((tm, tn), lambda j, i: (i, j))],
        out_specs=pl.BlockSpec((tm, tn), lambda j, i: (i, j)),
        out_shape=jax.ShapeDtypeStruct((m, n), F32),
        scratch_shapes=[pltpu.VMEM((k, tn), BF16)],
        compiler_params=_params(("arbitrary", "arbitrary"), blocks, _nbytes((k, tn), BF16)),
        name="mm_residual",
    )(x, w, res)


def _mm_swiglu(x, wg, wu, layer, t):
    m, k = x.shape
    n = wg.shape[2]
    tm, tn = _mm_tiles(t, k, n, 2)
    blocks = (_nbytes((tm, k), BF16) + 2 * _nbytes((k, tn), F32) + _nbytes((tm, tn), BF16))
    return pl.pallas_call(
        _mm_swiglu_kernel,
        grid=(n // tn, m // tm),
        in_specs=[pl.BlockSpec((tm, k), lambda j, i: (i, 0)),
                  pl.BlockSpec((None, k, tn), lambda j, i: (layer, 0, j)),
                  pl.BlockSpec((None, k, tn), lambda j, i: (layer, 0, j))],
        out_specs=pl.BlockSpec((tm, tn), lambda j, i: (i, j)),
        out_shape=jax.ShapeDtypeStruct((m, n), BF16),
        scratch_shapes=[pltpu.VMEM((k, tn), BF16), pltpu.VMEM((k, tn), BF16)],
        compiler_params=_params(("arbitrary", "arbitrary"), blocks, 2 * _nbytes((k, tn), BF16)),
        name="mm_swiglu",
    )(x, wg, wu)


def _conv_rows(load_rows, w_ref, carry_ref, n_rows, first_tile, emit):
    @pl.when(first_tile)
    def _():
        carry_ref[...] = jnp.zeros_like(carry_ref)

    width = w_ref.shape[0]
    n_sub = n_rows // CONV_SUB
    unroll = max(u for u in range(1, CONV_UNROLL + 1) if n_sub % u == 0)
    taps = [jnp.broadcast_to(w_ref[j:j + 1, :], (CONV_SUB, w_ref.shape[1])) for j in range(width)]

    def body(i, prev):
        for k in range(unroll):
            r0 = pl.multiple_of((i * unroll + k) * CONV_SUB, CONV_SUB)
            cur = load_rows(r0)
            xe = jnp.concatenate([prev, cur], axis=0)
            y = taps[width - 1] * cur
            for s in range(1, width):
                y = y + taps[width - 1 - s] * xe[HALO - s:HALO - s + CONV_SUB]
            emit(r0, y)
            prev = cur[CONV_SUB - HALO:]
        return prev

    carry_ref[...] = lax.fori_loop(0, n_sub // unroll, body, carry_ref[...])


def _sconv_kernel(b_ref, c_ref, u_ref, w_ref, o_ref, carry_ref, *, tiles_per_seq):
    def load_rows(r0):
        rows = pl.ds(r0, CONV_SUB)
        return c_ref[rows, :].astype(F32) * u_ref[rows, :].astype(F32)

    def emit(r0, y):
        rows = pl.ds(r0, CONV_SUB)
        o_ref[rows, :] = (b_ref[rows, :].astype(F32) * y).astype(o_ref.dtype)

    _conv_rows(load_rows, w_ref, carry_ref, o_ref.shape[0],
               pl.program_id(1) % tiles_per_seq == 0, emit)


def _sconv_gate(bcu, conv_w, t):
    m = bcu.shape[0]
    d = bcu.shape[1] // 3
    tr = _row_tile(t, CONV_ROW_TILE)
    tc = CONV_COL_TILE
    nb = d // tc
    blocks = 4 * _nbytes((tr, tc), BF16) + _nbytes((SC_WIDTH, tc), F32)
    return pl.pallas_call(
        functools.partial(_sconv_kernel, tiles_per_seq=t // tr),
        grid=(nb, m // tr),
        in_specs=[pl.BlockSpec((tr, tc), lambda j, i: (i, j)),
                  pl.BlockSpec((tr, tc), lambda j, i: (i, j + nb)),
                  pl.BlockSpec((tr, tc), lambda j, i: (i, j + 2 * nb)),
                  pl.BlockSpec((SC_WIDTH, tc), lambda j, i: (0, j))],
        out_specs=pl.BlockSpec((tr, tc), lambda j, i: (i, j)),
        out_shape=jax.ShapeDtypeStruct((m, d), BF16),
        scratch_shapes=[pltpu.VMEM((HALO, tc), F32)],
        compiler_params=_params(("arbitrary", "arbitrary"), blocks),
        name="sconv_gate",
    )(bcu, bcu, bcu, conv_w.astype(F32))


def _gdn_conv_kernel(x_ref, w_ref, o_ref, carry_ref, *, tiles_per_seq, n_l2_blocks, n_q_blocks):
    j = pl.program_id(0)
    first_tile = pl.program_id(1) % tiles_per_seq == 0

    def load_rows(r0):
        return x_ref[pl.ds(r0, CONV_SUB), :].astype(F32)

    def emit_plain(r0, y):
        o_ref[pl.ds(r0, CONV_SUB), :] = (y * jax.nn.sigmoid(y)).astype(o_ref.dtype)

    def emit_l2(r0, y):
        y = y * jax.nn.sigmoid(y)
        scale = jnp.where(j < n_q_blocks, GDN_DK ** -0.5, 1.0).astype(F32)
        for hh in range(y.shape[1] // GDN_DK):
            ys = y[:, hh * GDN_DK:(hh + 1) * GDN_DK]
            inv = lax.rsqrt(jnp.sum(ys * ys, axis=-1, keepdims=True) + L2_EPS) * scale
            o_ref[pl.ds(r0, CONV_SUB), hh * GDN_DK:(hh + 1) * GDN_DK] = (ys * inv).astype(o_ref.dtype)

    @pl.when(j < n_l2_blocks)
    def _():
        _conv_rows(load_rows, w_ref, carry_ref, o_ref.shape[0], first_tile, emit_l2)

    @pl.when(j >= n_l2_blocks)
    def _():
        _conv_rows(load_rows, w_ref, carry_ref, o_ref.shape[0], first_tile, emit_plain)


def _gdn_conv(proj, conv_w, t, out_dtype):
    m = proj.shape[0]
    tr = _row_tile(t, CONV_ROW_TILE)
    tc = CONV_COL_TILE
    blocks = _nbytes((tr, tc), proj.dtype) + _nbytes((tr, tc), out_dtype) + _nbytes((GDN_CONV_WIDTH, tc), F32)
    return pl.pallas_call(
        functools.partial(_gdn_conv_kernel, tiles_per_seq=t // tr,
                          n_l2_blocks=2 * GDN_QK_DIM // tc, n_q_blocks=GDN_QK_DIM // tc),
        grid=(GDN_CONV_DIM // tc, m // tr),
        in_specs=[pl.BlockSpec((tr, tc), lambda j, i: (i, j)),
                  pl.BlockSpec((GDN_CONV_WIDTH, tc), lambda j, i: (0, j))],
        out_specs=pl.BlockSpec((tr, tc), lambda j, i: (i, j)),
        out_shape=jax.ShapeDtypeStruct((m, GDN_CONV_DIM), out_dtype),
        scratch_shapes=[pltpu.VMEM((HALO, tc), F32)],
        compiler_params=_params(("arbitrary", "arbitrary"), blocks),
        name="gdn_conv",
    )(proj, conv_w.astype(F32))


def _gdn_gates_kernel(ba_ref, alog_ref, dtb_ref, o_ref):
    x = ba_ref[...]
    z = x + dtb_ref[...]
    softplus = jnp.maximum(z, 0.0) + jnp.log1p(jnp.exp(-jnp.abs(z)))
    g = -jnp.exp(alog_ref[...]) * softplus
    pos = lax.broadcasted_iota(jnp.int32, x.shape, 0) % CHUNK
    s = 1
    while s < CHUNK:
        g = g + jnp.where(pos >= s, pltpu.roll(g, s, 0), 0.0)
        s *= 2
    lane = lax.broadcasted_iota(jnp.int32, x.shape, 1)
    o_ref[...] = jnp.where(lane < GDN_NV, jax.nn.sigmoid(x), g)


def _gdn_gates(ba, a_log, dt_bias, t):
    m = ba.shape[0]
    tr = _row_tile(t, 1040, CHUNK)
    pad = LANES - 2 * GDN_NV
    alog = jnp.concatenate([jnp.zeros((GDN_NV,), F32), a_log.astype(F32), jnp.zeros((pad,), F32)])
    dtb = jnp.concatenate([jnp.zeros((GDN_NV,), F32), dt_bias.astype(F32), jnp.zeros((pad,), F32)])
    return pl.pallas_call(
        _gdn_gates_kernel,
        grid=(m // tr,),
        in_specs=[pl.BlockSpec((tr, LANES), lambda i: (i, 0)),
                  pl.BlockSpec((1, LANES), lambda i: (0, 0)),
                  pl.BlockSpec((1, LANES), lambda i: (0, 0))],
        out_specs=pl.BlockSpec((tr, LANES), lambda i: (i, 0)),
        out_shape=jax.ShapeDtypeStruct((m, LANES), F32),
        compiler_params=_params(("arbitrary",), 2 * _nbytes((tr, LANES), F32)),
        name="gdn_gates",
    )(ba, alog.reshape(1, LANES), dtb.reshape(1, LANES))


def _bdot(a, b):
    return jnp.dot(a.astype(BF16), b.astype(BF16), preferred_element_type=F32)


def _gdn_chunk_kernel(q_ref, k_ref, v_ref, z_ref, col_ref, row_ref, nw_ref, o_ref, s_ref):
    n_chunks = q_ref.shape[0] // CHUNK
    n_heads = HEADS_PER_STEP
    rep = n_heads // KHEADS_PER_STEP
    chunks = range(n_chunks)
    heads = range(n_heads)
    pairs = [(c, h) for c in chunks for h in heads]

    @pl.when(pl.program_id(2) == 0)
    def _():
        s_ref[...] = jnp.zeros_like(s_ref)

    ci = lax.broadcasted_iota(jnp.int32, (CHUNK, rep * CHUNK), 0)
    lane = lax.broadcasted_iota(jnp.int32, (CHUNK, rep * CHUNK), 1)
    si = lane & (CHUNK - 1)
    first = lane < CHUNK
    causal = ci >= si
    strict = ci > si
    blk16 = (ci >> 4) == (si >> 4)
    blk32 = (ci >> 5) == (si >> 5)
    eye = (ci == si).astype(F32)
    nw = nw_ref[...]

    def rows(c):
        return slice(c * CHUNK, (c + 1) * CHUNK)

    def hcols(h, width):
        return slice(h * width, (h + 1) * width)

    def bdiag(x):
        xb = x.astype(BF16)
        zero = jnp.zeros_like(xb)
        return jnp.concatenate([jnp.where(first, xb, zero), jnp.where(first, zero, xb)], axis=0)

    def pdot(a, b):
        return jnp.dot(a.astype(BF16), b, preferred_element_type=F32)

    khs = range(KHEADS_PER_STEP)
    wide = (CHUNK, GDN_DK)
    assert rep == 2
    rec = {"state": [s_ref[h] for h in heads]}

    def chunk_stages(c):
        v = {}

        def side_by_side(d, kh):
            return jnp.where(first, d[kh * rep][:, :rep * CHUNK], d[kh * rep + 1][:, :rep * CHUNK])

        def head_of(d2, h):
            return d2[h // rep][:, (h % rep) * CHUNK:(h % rep + 1) * CHUNK]

        def prep():
            v["qc"] = [q_ref[rows(c), hcols(kh, GDN_DK)].astype(F32) for kh in khs]
            v["kc"] = [k_ref[rows(c), hcols(kh, GDN_DK)].astype(F32) for kh in khs]
            v["beta"] = [jnp.broadcast_to(col_ref[0, rows(c), h:h + 1], wide) for h in heads]
            v["gcol"] = [jnp.broadcast_to(col_ref[0, rows(c), n_heads + h:n_heads + h + 1], wide)
                         for h in heads]
            v["glast"] = [g[CHUNK - 1:CHUNK, :] for g in v["gcol"]]
            v["eg"] = [jnp.exp(g) for g in v["gcol"]]
            v["kb"] = [v["kc"][h // rep] * v["beta"][h] for h in heads]
            v["kk"] = []
            for kh in khs:
                lhs = jnp.concatenate([v["kc"][kh], v["qc"][kh]], axis=0).astype(BF16)
                k2 = jnp.concatenate([v["kc"][kh]] * rep, axis=0).astype(BF16)
                v["kk"].append(lax.dot_general(lhs, k2, _NT_DIMS, preferred_element_type=F32))

        def neumann_start():
            decay = [jnp.where(causal, jnp.exp(side_by_side(v["gcol"], kh) - row_ref[0, c, kh:kh + 1, :]), 0.0)
                     for kh in khs]
            nmat = [jnp.where(strict, -(v["kk"][kh][:CHUNK] * side_by_side(v["beta"], kh)) * decay[kh], 0.0)
                    for kh in khs]
            v["qk2"] = [jnp.where(causal, v["kk"][kh][CHUNK:] * decay[kh], 0.0).astype(BF16) for kh in khs]
            nd = [jnp.where(blk16, n, 0.0) for n in nmat]
            v["no1"] = [(jnp.where(blk32, nmat[kh], 0.0) - nd[kh]).astype(BF16) for kh in khs]
            v["no2_2"] = [jnp.where(blk32, 0.0, n).astype(BF16) for n in nmat]
            v["pw"] = [pdot(n, bdiag(n)) for n in nd]
            v["sm"] = [eye + n for n in nd]

        def neumann_double():
            r = [pdot(v["pw"][kh], jnp.concatenate([bdiag(v["pw"][kh]), bdiag(v["sm"][kh])], axis=1))
                 for kh in khs]
            v["pw"] = [x[:, :rep * CHUNK] for x in r]
            v["sm"] = [v["sm"][kh] + r[kh][:, rep * CHUNK:] for kh in khs]

        def neumann_end():
            v["t16"] = [v["sm"][kh] + pdot(v["pw"][kh], bdiag(v["sm"][kh])) for kh in khs]

        def merge_a():
            v["x1"] = [jnp.dot(v["no1"][kh], bdiag(v["t16"][kh]), preferred_element_type=F32) for kh in khs]

        def merge_b():
            tm2 = [(v["t16"][kh] + pdot(v["t16"][kh], bdiag(v["x1"][kh])) - eye).astype(BF16)
                   for kh in khs]
            v["tm"] = [head_of(tm2, h) for h in heads]
            v["no2"] = [head_of(v["no2_2"], h) for h in heads]
            v["qk"] = [head_of(v["qk2"], h) for h in heads]

        def solve_a():
            rhs = [jnp.concatenate([v_ref[rows(c), hcols(h, GDN_DV)].astype(F32) * v["beta"][h],
                                    v["kb"][h] * v["eg"][h]], axis=1) for h in heads]
            v["y"] = [rhs[h] + jnp.dot(v["tm"][h], rhs[h].astype(BF16), preferred_element_type=F32)
                      for h in heads]

        def solve_b():
            v["zz"] = [jnp.dot(v["no2"][h], v["y"][h].astype(BF16), preferred_element_type=F32)
                       for h in heads]

        def solve_c():
            uw = [v["y"][h] + v["zz"][h] + jnp.dot(v["tm"][h], v["zz"][h].astype(BF16),
                                                     preferred_element_type=F32) for h in heads]
            v["u"] = [x[:, :GDN_DV] for x in uw]
            v["wq"] = [jnp.concatenate([uw[h][:, GDN_DV:], v["qc"][h // rep] * v["eg"][h]],
                                       axis=0).astype(BF16) for h in heads]
            v["ktail"] = [(v["kc"][h // rep] * jnp.exp(v["glast"][h] - v["gcol"][h])).astype(BF16)
                          for h in heads]
            v["eglast"] = [jnp.exp(g) for g in v["glast"]]

        def recur_a():
            v["ws_qs"] = [jnp.dot(v["wq"][h], rec["state"][h].astype(BF16), preferred_element_type=F32)
                          for h in heads]

        def recur_b():
            ws_qs = v["ws_qs"]
            v_new = [(v["u"][h] - ws_qs[h][:CHUNK]).astype(BF16) for h in heads]
            o2 = [jnp.dot(v["qk"][h], v_new[h], preferred_element_type=F32) for h in heads]
            ds = [lax.dot_general(v["ktail"][h], v_new[h], (((0,), (0,)), ((), ())),
                                  preferred_element_type=F32) for h in heads]
            rec["state"] = [rec["state"][h] * v["eglast"][h] + ds[h] for h in heads]
            for h in heads:
                o = ws_qs[h][CHUNK:] + o2[h]
                zc = z_ref[rows(c), hcols(h, GDN_DV)].astype(F32)
                on = o * lax.rsqrt(jnp.mean(o * o, axis=-1, keepdims=True) + NORM_EPS) * nw
                o_ref[rows(c), hcols(h, GDN_DV)] = (on * (zc * jax.nn.sigmoid(zc))).astype(o_ref.dtype)

        return [prep, neumann_start, neumann_double, neumann_double, neumann_end, merge_a, merge_b,
                solve_a, solve_b, solve_c, recur_a, recur_b]

    stages = [chunk_stages(c) for c in chunks]
    n_stages = len(stages[0])
    assert GDN_STAGE_SKEW >= 2
    for t in range(n_stages + GDN_STAGE_SKEW * (n_chunks - 1)):
        for c in chunks:
            j = t - GDN_STAGE_SKEW * c
            if 0 <= j < n_stages:
                stages[c][j]()
    for h in heads:
        s_ref[h] = rec["state"][h]


def _gdn_chunk(qkv, proj, gates, norm_w, bsz, t):
    m = qkv.shape[0]
    hs, khs = HEADS_PER_STEP, KHEADS_PER_STEP
    n_groups = GDN_NV // hs
    rb = _row_tile(t, GDN_CHUNKS_PER_STEP * CHUNK, CHUNK)
    nrb = t // rb
    ncb = rb // CHUNK
    n_chunks_total = m // CHUNK
    beta = gates[:, :GDN_NV].reshape(m, n_groups, hs)
    gc = gates[:, GDN_NV:2 * GDN_NV].reshape(m, n_groups, hs)
    col = jnp.transpose(jnp.concatenate([beta, gc], axis=-1), (1, 0, 2))
    row = jnp.transpose(gc.reshape(n_chunks_total, CHUNK, n_groups, hs), (2, 0, 3, 1))
    row = row.reshape(n_groups, n_chunks_total, khs, (hs // khs) * CHUNK)

    kq = khs * GDN_DK
    vq = hs * GDN_DV
    q_off = 0
    k_off = GDN_QK_DIM // kq
    v_off = 2 * GDN_QK_DIM // vq
    z_off = GDN_CONV_DIM // vq
    blocks = (2 * _nbytes((rb, kq), qkv.dtype) + _nbytes((rb, vq), qkv.dtype)
              + _nbytes((rb, vq), proj.dtype) + _nbytes((rb, LANES), F32)
              + _nbytes((ncb, SUBLANES, LANES), F32) + _nbytes((rb, vq), BF16))
    return pl.pallas_call(
        _gdn_chunk_kernel,
        grid=(bsz, n_groups, nrb),
        in_specs=[pl.BlockSpec((rb, kq), lambda b, g, r: (b * nrb + r, q_off + g)),
                  pl.BlockSpec((rb, kq), lambda b, g, r: (b * nrb + r, k_off + g)),
                  pl.BlockSpec((rb, vq), lambda b, g, r: (b * nrb + r, v_off + g)),
                  pl.BlockSpec((rb, vq), lambda b, g, r: (b * nrb + r, z_off + g)),
                  pl.BlockSpec((1, rb, 2 * hs), lambda b, g, r: (g, b * nrb + r, 0)),
                  pl.BlockSpec((1, ncb, khs, (hs // khs) * CHUNK), lambda b, g, r: (g, b * nrb + r, 0, 0)),
                  pl.BlockSpec((1, GDN_DV), lambda b, g, r: (0, 0))],
        out_specs=pl.BlockSpec((rb, vq), lambda b, g, r: (b * nrb + r, g)),
        out_shape=jax.ShapeDtypeStruct((m, GDN_V_DIM), BF16),
        scratch_shapes=[pltpu.VMEM((hs, GDN_DK, GDN_DV), F32)],
        compiler_params=_params(("arbitrary", "arbitrary", "arbitrary"), blocks,
                                _nbytes((hs, GDN_DK, GDN_DV), F32)),
        name="gdn_chunk",
    )(qkv, qkv, qkv, proj, col, row, norm_w.reshape(1, GDN_DV).astype(F32))


def _short_conv_mixer(h, hn, w_in, conv_w, w_out, j, t):
    bcu = _mm_plain(hn, w_in, j, BF16, t)
    y = _sconv_gate(bcu, conv_w[j], t)
    return _mm_residual(y, w_out, j, h, t)


def _gated_deltanet_mixer(h, hn, w_in, conv_w, a_log, dt_bias, norm_w, w_out, j, bsz, t):
    main_cols = GDN_CONV_DIM + GDN_V_DIM
    w_in_t = jnp.swapaxes(w_in, 1, 2)
    proj = _mm_plain_t(hn, w_in_t, j, BF16, t, main_cols)
    ba = _mm_tail_t(hn, w_in_t, j, main_cols, t)
    gates = _gdn_gates(ba, a_log[j], dt_bias[j], t)
    qkv = _gdn_conv(proj, conv_w[j], t, BF16)
    og = _gdn_chunk(qkv, proj, gates, norm_w[j], bsz, t)
    return _mm_residual(og, w_out, j, h, t)


def _ffn(h, hn, w_gate, w_up, w_down, i, t):
    a = _mm_swiglu(hn, w_gate, w_up, i, t)
    return _mm_residual(a, w_down, i, h, t)


def kernel(x, meta_tokens, mixer_norm, ffn_norm, sc_w_in, sc_conv_w, sc_w_out, gdn_w_in, gdn_conv_w, gdn_a_log, gdn_dt_bias, gdn_norm_w, gdn_w_out, ffn_w_gate, ffn_w_up, ffn_w_down, final_norm):
    bsz, seq, d = x.shape
    depth = mixer_norm.shape[0]
    t = PAD_FRONT + N_META + seq
    assert t % CHUNK == 0
    h, hn = _embed(x, meta_tokens, mixer_norm[0], t)
    for i in range(depth):
        j = i // 2
        if i > 0:
            hn = _rmsnorm(h, mixer_norm[i], BF16, t)
        if i % 2 == 0:
            h = _short_conv_mixer(h, hn, sc_w_in, sc_conv_w, sc_w_out, j, t)
        else:
            h = _gated_deltanet_mixer(h, hn, gdn_w_in, gdn_conv_w, gdn_a_log, gdn_dt_bias,
                                      gdn_norm_w, gdn_w_out, j, bsz, t)
        hn = _rmsnorm(h, ffn_norm[i], BF16, t)
        h = _ffn(h, hn, ffn_w_gate, ffn_w_up, ffn_w_down, i, t)
    return _final_rmsnorm(h, final_norm, bsz, t, seq)
```
